```python
import jax, jax.numpy as jnp
from jax import lax
import numpy as np


D_MODEL = 2048
BATCH = 1
SEQ = 8192
DEPTH = 1

CHUNK = 64
N_META = 16
PAD_FRONT = (-N_META) % CHUNK
MIX_WIDTH = D_MODEL
HG_WIDTH = MIX_WIDTH // 2
RET_WIDTH = MIX_WIDTH - HG_WIDTH
HG_HEAD_DIM = 128
HG_HEADS = HG_WIDTH // HG_HEAD_DIM
RET_HEADS = 4
RET_HEAD_DIM = RET_WIDTH // RET_HEADS
D_FF = ((8 * D_MODEL + 3 * 256 - 1) // (3 * 256)) * 256
ROPE_BASE = 10000.0
LN_EPS = 1e-5
HEAD_NORM_EPS = 1e-6
DEEPNORM_ALPHA = (2.0 * DEPTH) ** 0.25
DEEPNORM_BETA = (8.0 * DEPTH) ** -0.25
PROJ_WIDTHS = (HG_WIDTH,) * 4 + (RET_WIDTH,) * 4
PROJ_SPLITS = tuple(int(s) for s in np.cumsum(PROJ_WIDTHS)[:-1])
N_PROJ = sum(PROJ_WIDTHS)

kernel_name = 'hybrid_hgrn2_retention_deepnorm_encoder'


def layer_norm(x, g, b):
    xf = x.astype(jnp.float32)
    mu = jnp.mean(xf, axis=-1, keepdims=True)
    var = jnp.mean(jnp.square(xf - mu), axis=-1, keepdims=True)
    y = (xf - mu) * lax.rsqrt(var + LN_EPS)
    return (y * g.astype(jnp.float32) + b.astype(jnp.float32)).astype(x.dtype)


def rotary(x, pos):
    half = x.shape[-1] // 2
    inv_freq = ROPE_BASE ** (-jnp.arange(half, dtype=jnp.float32) / half)
    ang = pos.astype(jnp.float32)[:, None] * inv_freq[None, :]
    cos = jnp.cos(ang)[None, :, None, :]
    sin = jnp.sin(ang)[None, :, None, :]
    x1, x2 = x[..., :half], x[..., half:]
    return jnp.concatenate([x1 * cos - x2 * sin, x1 * sin + x2 * cos], axis=-1)


def hgrn2_group(q, f_pre, i, g, lb, valid, norm_g):
    b_sz, length, _ = q.shape
    n_chunks = length // CHUNK
    f32 = jnp.float32
    m = valid[None, :, None]
    q = jax.nn.silu(q.astype(f32)) * (HG_HEAD_DIM ** -0.5)
    forget = lb + (1.0 - lb) * jax.nn.sigmoid(f_pre.astype(f32))
    log_f = jnp.where(m, jnp.log(forget), 0.0)
    k = jnp.where(m, 1.0 - forget, 0.0)
    v = i.astype(f32)

    def to_chunks(t):
        return t.reshape(b_sz, n_chunks, CHUNK, HG_HEADS, HG_HEAD_DIM).transpose(1, 0, 3, 2, 4)

    causal = jnp.tril(jnp.ones((CHUNK, CHUNK), dtype=bool))[:, :, None]

    def step(state, inp):
        qc, kc, vc, gc = inp
        cum = jnp.cumsum(gc, axis=2)
        diff = cum[:, :, :, None, :] - cum[:, :, None, :, :]
        decay = jnp.exp(jnp.where(causal, diff, -jnp.inf))
        scores = jnp.einsum('bhtk,bhsk,bhtsk->bhts', qc, kc, decay)
        out = (jnp.einsum('bhts,bhsv->bhtv', scores, vc)
               + jnp.einsum('bhtk,bhkv->bhtv', qc * jnp.exp(cum), state))
        last = cum[:, :, -1:, :]
        state = (jnp.exp(last[:, :, 0, :])[..., None] * state
                 + jnp.einsum('bhsk,bhsv->bhkv', kc * jnp.exp(last - cum), vc))
        return state, out

    s0 = jnp.zeros((b_sz, HG_HEADS, HG_HEAD_DIM, HG_HEAD_DIM), f32)
    _, o = lax.scan(step, s0, (to_chunks(q), to_chunks(k), to_chunks(v), to_chunks(log_f)))
    o = o.transpose(1, 0, 3, 2, 4).reshape(b_sz, length, HG_HEADS, HG_HEAD_DIM)
    o = o * lax.rsqrt(jnp.mean(jnp.square(o), axis=-1, keepdims=True) + HEAD_NORM_EPS)
    o = o.reshape(b_sz, length, HG_WIDTH) * norm_g.astype(f32)
    return o * jax.nn.silu(g.astype(f32))


def retention_group(q, k, v, g, pos, valid, norm_g, norm_b):
    b_sz, length, _ = q.shape
    n_chunks = length // CHUNK
    f32 = jnp.float32
    qh = rotary(q.astype(f32).reshape(b_sz, length, RET_HEADS, RET_HEAD_DIM), pos)
    kh = rotary(k.astype(f32).reshape(b_sz, length, RET_HEADS, RET_HEAD_DIM), pos) * (RET_HEAD_DIM ** -0.5)
    kh = jnp.where(valid[None, :, None, None], kh, 0.0)
    vh = v.astype(f32).reshape(b_sz, length, RET_HEADS, RET_HEAD_DIM)
    log_gamma = jnp.log(1.0 - 2.0 ** (-5.0 - jnp.arange(RET_HEADS, dtype=f32)))

    def to_chunks(t):
        return t.reshape(b_sz, n_chunks, CHUNK, RET_HEADS, RET_HEAD_DIM).transpose(0, 3, 1, 2, 4)

    qc, kc, vc = to_chunks(qh), to_chunks(kh), to_chunks(vh)
    j = jnp.arange(CHUNK, dtype=f32)
    rel = j[:, None] - j[None, :]
    intra_decay = jnp.where(rel[None] >= 0,
                            jnp.exp(jnp.maximum(rel, 0.0)[None] * log_gamma[:, None, None]), 0.0)
    scores = jnp.einsum('bhntd,bhnsd->bhnts', qc, kc) * intra_decay[None, :, None]
    o = jnp.einsum('bhnts,bhnsv->bhntv', scores, vc)
    k_dec = kc * jnp.exp((CHUNK - 1.0 - j)[None, :] * log_gamma[:, None])[None, :, None, :, None]
    kv = jnp.einsum('bhnsd,bhnsv->nbhdv', k_dec, vc)
    chunk_decay = jnp.exp(CHUNK * log_gamma)[None, :, None, None]

    def step(state, kv_n):
        return chunk_decay * state + kv_n, state

    r0 = jnp.zeros((b_sz, RET_HEADS, RET_HEAD_DIM, RET_HEAD_DIM), f32)
    _, r_prev = lax.scan(step, r0, kv)
    q_dec = qc * jnp.exp((j + 1.0)[None, :] * log_gamma[:, None])[None, :, None, :, None]
    o = o + jnp.einsum('bhntd,nbhdv->bhntv', q_dec, r_prev)
    o = o.transpose(0, 2, 3, 1, 4).reshape(b_sz, length, RET_HEADS, RET_HEAD_DIM)
    mu = jnp.mean(o, axis=-1, keepdims=True)
    var = jnp.mean(jnp.square(o - mu), axis=-1, keepdims=True)
    o = ((o - mu) * lax.rsqrt(var + HEAD_NORM_EPS)).reshape(b_sz, length, RET_WIDTH)
    o = o * norm_g.astype(f32) + norm_b.astype(f32)
    return o * jax.nn.silu(g.astype(f32))


def setup_inputs(seed: int = 0) -> dict:
    key = jax.random.key(seed)
    ks = jax.random.split(key, 20)
    f32 = jnp.float32
    nrm = lambda k, shape: jax.random.normal(k, shape, f32)
    beta = DEEPNORM_BETA
    col_scale = np.concatenate([np.full((w,), s, np.float32) for w, s in zip(
        PROJ_WIDTHS, (1.0, 1.0, beta, 1.0, 1.0, 1.0, beta, 1.0))])
    x = nrm(ks[0], (BATCH, SEQ, D_MODEL))
    meta_tokens = nrm(ks[1], (N_META, D_MODEL))
    ln_in_g = 1.0 + 0.02 * nrm(ks[2], (D_MODEL,))
    ln_in_b = 0.02 * nrm(ks[3], (D_MODEL,))
    hg_lower_bounds = 0.1 * nrm(ks[4], (DEPTH + 1, HG_WIDTH))
    w_in = nrm(ks[5], (DEPTH, D_MODEL, N_PROJ)) * (D_MODEL ** -0.5) * jnp.asarray(col_scale)
    hg_norm_g = 1.0 + 0.02 * nrm(ks[6], (DEPTH, HG_WIDTH))
    ret_norm_g = 1.0 + 0.02 * nrm(ks[7], (DEPTH, RET_WIDTH))
    ret_norm_b = 0.02 * nrm(ks[8], (DEPTH, RET_WIDTH))
    w_out = nrm(ks[9], (DEPTH, MIX_WIDTH, D_MODEL)) * (MIX_WIDTH ** -0.5) * beta
    ln1_g = 1.0 + 0.02 * nrm(ks[10], (DEPTH, D_MODEL))
    ln1_b = 0.02 * nrm(ks[11], (DEPTH, D_MODEL))
    w_gate = nrm(ks[12], (DEPTH, D_MODEL, D_FF)) * (D_MODEL ** -0.5) * beta
    w_up = nrm(ks[13], (DEPTH, D_MODEL, D_FF)) * (D_MODEL ** -0.5) * beta
    w_down = nrm(ks[14], (DEPTH, D_FF, D_MODEL)) * (D_FF ** -0.5) * beta
    ln2_g = 1.0 + 0.02 * nrm(ks[15], (DEPTH, D_MODEL))
    ln2_b = 0.02 * nrm(ks[16], (DEPTH, D_MODEL))
    return {'x': x, 'meta_tokens': meta_tokens, 'ln_in_g': ln_in_g, 'ln_in_b': ln_in_b,
            'hg_lower_bounds': hg_lower_bounds, 'w_in': w_in, 'hg_norm_g': hg_norm_g,
            'ret_norm_g': ret_norm_g, 'ret_norm_b': ret_norm_b, 'w_out': w_out,
            'ln1_g': ln1_g, 'ln1_b': ln1_b, 'w_gate': w_gate, 'w_up': w_up, 'w_down': w_down,
            'ln2_g': ln2_g, 'ln2_b': ln2_b}


def reference(x, meta_tokens, ln_in_g, ln_in_b, hg_lower_bounds, w_in, hg_norm_g, ret_norm_g,
              ret_norm_b, w_out, ln1_g, ln1_b, w_gate, w_up, w_down, ln2_g, ln2_b):
    b_sz, seq, _ = x.shape
    length = PAD_FRONT + N_META + seq
    pad = jnp.zeros((b_sz, PAD_FRONT, D_MODEL), x.dtype)
    meta = jnp.broadcast_to(meta_tokens.astype(x.dtype)[None], (b_sz, N_META, D_MODEL))
    h = layer_norm(jnp.concatenate([pad, meta, x], axis=1), ln_in_g, ln_in_b)
    pos = jnp.arange(length, dtype=jnp.int32) - PAD_FRONT
    valid = pos >= 0
    lb_all = jnp.cumsum(jax.nn.softmax(hg_lower_bounds.astype(jnp.float32), axis=0), axis=0)
    for layer in range(DEPTH):
        proj = h @ w_in[layer]
        hg_q, hg_f, hg_i, hg_g, r_q, r_k, r_v, r_g = jnp.split(proj, PROJ_SPLITS, axis=-1)
        o_hg = hgrn2_group(hg_q, hg_f, hg_i, hg_g, lb_all[layer], valid, hg_norm_g[layer])
        o_ret = retention_group(r_q, r_k, r_v, r_g, pos, valid, ret_norm_g[layer], ret_norm_b[layer])
        mix = jnp.concatenate([o_hg, o_ret], axis=-1).astype(h.dtype) @ w_out[layer]
        h = layer_norm(DEEPNORM_ALPHA * h + mix, ln1_g[layer], ln1_b[layer])
        ff = (jax.nn.silu(h @ w_gate[layer]) * (h @ w_up[layer])) @ w_down[layer]
        h = layer_norm(DEEPNORM_ALPHA * h + ff, ln2_g[layer], ln2_b[layer])
    return h[:, PAD_FRONT + N_META:, :]
```

```python
import functools

import numpy as np
import jax
import jax.numpy as jnp
from jax import lax
from jax.experimental import pallas as pl
from jax.experimental.pallas import tpu as pltpu

D_MODEL = 2048
CHUNK = 64
N_META = 16
PAD_FRONT = (-N_META) % CHUNK
HG_WIDTH = 1024
RET_WIDTH = 1024
HG_HEAD_DIM = 128
HG_HEADS = HG_WIDTH // HG_HEAD_DIM
RET_HEADS = 4
RET_HEAD_DIM = RET_WIDTH // RET_HEADS
ROT_HALF = RET_HEAD_DIM // 2
D_FF = 5632
N_PROJ = 4 * HG_WIDTH + 4 * RET_WIDTH
ROPE_BASE = 10000.0
LN_EPS = 1e-5
HEAD_NORM_EPS = 1e-6
DEPTH = 1
DEEPNORM_ALPHA = (2.0 * DEPTH) ** 0.25

F32 = jnp.float32
BF16 = jnp.bfloat16

VMEM_LIMIT_BYTES = 56 * 1024 * 1024

_NT = (((1,), (1,)), ((), ()))


def _layer_norm(x, g, b):
    mu = jnp.mean(x, axis=-1, keepdims=True)
    xc = x - mu
    var = jnp.mean(xc * xc, axis=-1, keepdims=True)
    return xc * lax.rsqrt(var + LN_EPS) * g + b


def _silu(x):
    return x * jax.nn.sigmoid(x)


def _ln_proj_kernel(x_ref, g_ref, b_ref, w_ref, o_ref, hb_ref, *, row_sub):
    @pl.when(pl.program_id(1) == 0)
    def _():
        for r in range(0, x_ref.shape[0], row_sub):
            rows = pl.ds(r, row_sub)
            hb_ref[rows, :] = _layer_norm(x_ref[rows, :], g_ref[...], b_ref[...]).astype(BF16)

    o_ref[...] = jnp.dot(hb_ref[...], w_ref[...], preferred_element_type=F32)


def _ln_proj(x, g, b, w_bf16, *, tm, tn):
    m = x.shape[0]
    row_sub = min(tm, 128)
    return pl.pallas_call(
        functools.partial(_ln_proj_kernel, row_sub=row_sub),
        grid=(m // tm, N_PROJ // tn),
        in_specs=[
            pl.BlockSpec((tm, D_MODEL), lambda i, j: (i, 0)),
            pl.BlockSpec((1, D_MODEL), lambda i, j: (0, 0)),
            pl.BlockSpec((1, D_MODEL), lambda i, j: (0, 0)),
            pl.BlockSpec((D_MODEL, tn), lambda i, j: (0, j)),
        ],
        out_specs=pl.BlockSpec((tm, tn), lambda i, j: (i, j)),
        out_shape=jax.ShapeDtypeStruct((m, N_PROJ), F32),
        scratch_shapes=[pltpu.VMEM((tm, D_MODEL), BF16)],
        compiler_params=pltpu.CompilerParams(
            dimension_semantics=("parallel", "arbitrary"),
            vmem_limit_bytes=VMEM_LIMIT_BYTES),
        name="ln_proj",
    )(x, g, b, w_bf16)


def _mixer_kernel(proj_ref, cos_ref, sin_ref, hlb_ref, hgn_ref, rng_ref, rnb_ref,
                  s_in_ref, r_in_ref, out_ref, s_out_ref, r_out_ref, s_ref, r_ref,
                  *, mask_pad):
    c = pl.program_id(0)

    @pl.when(c == 0)
    def _():
        s_ref[...] = s_in_ref[...]
        r_ref[...] = r_in_ref[...]

    row = lax.broadcasted_iota(jnp.int32, (CHUNK, 1), 0)
    valid = row >= PAD_FRONT
    t_idx = lax.broadcasted_iota(jnp.int32, (CHUNK, CHUNK), 0)
    s_idx = lax.broadcasted_iota(jnp.int32, (CHUNK, CHUNK), 1)
    causal = t_idx >= s_idx

    q = proj_ref[:, 0 * HG_WIDTH:1 * HG_WIDTH]
    f_pre = proj_ref[:, 1 * HG_WIDTH:2 * HG_WIDTH]
    v = proj_ref[:, 2 * HG_WIDTH:3 * HG_WIDTH]
    g = proj_ref[:, 3 * HG_WIDTH:4 * HG_WIDTH]

    a0 = hlb_ref[0:1, :]
    a1 = hlb_ref[1:2, :]
    amax = jnp.maximum(a0, a1)
    e0 = jnp.exp(a0 - amax)
    e1 = jnp.exp(a1 - amax)
    lb = e0 / (e0 + e1)

    qs = _silu(q) * (HG_HEAD_DIM ** -0.5)
    forget = lb + (1.0 - lb) * jax.nn.sigmoid(f_pre)
    log_f = jnp.log(forget)
    kk = 1.0 - forget
    if mask_pad:
        log_f = jnp.where(valid, log_f, 0.0)
        kk = jnp.where(valid, kk, 0.0)

    tri = jnp.where(causal, 1.0, 0.0).astype(BF16)
    lf_hi = log_f.astype(BF16)
    rem = log_f - lf_hi.astype(F32)
    lf_mid = rem.astype(BF16)
    lf_lo = (rem - lf_mid.astype(F32)).astype(BF16)
    cum = (jnp.dot(tri, lf_hi, preferred_element_type=F32)
           + jnp.dot(tri, lf_mid, preferred_element_type=F32)
           + jnp.dot(tri, lf_lo, preferred_element_type=F32))
    last = cum[CHUNK - 1:CHUNK, :]

    q_in = (qs * jnp.exp(cum)).astype(BF16)
    k_in = (kk * jnp.exp(-cum)).astype(BF16)
    k_out = kk * jnp.exp(last - cum)
    chunk_decay = jnp.exp(last)
    v_bf = v.astype(BF16)
    gate = _silu(g)

    for h in range(HG_HEADS):
        sl = slice(h * HG_HEAD_DIM, (h + 1) * HG_HEAD_DIM)
        state_t = s_ref[h]
        scores = lax.dot_general(q_in[:, sl], k_in[:, sl], _NT, preferred_element_type=F32)
        scores = jnp.where(causal, scores, 0.0).astype(BF16)
        o = (jnp.dot(scores, v_bf[:, sl], preferred_element_type=F32)
             + lax.dot_general(q_in[:, sl], state_t.astype(BF16), _NT, preferred_element_type=F32))
        v_t = v[:, sl].T.astype(BF16)
        s_ref[h] = (state_t * chunk_decay[:, sl]
                    + jnp.dot(v_t, k_out[:, sl].astype(BF16), preferred_element_type=F32))
        ms = jnp.mean(o * o, axis=-1, keepdims=True)
        o = o * lax.rsqrt(ms + HEAD_NORM_EPS) * hgn_ref[:, sl] * gate[:, sl]
        out_ref[:, sl] = o.astype(out_ref.dtype)

    rq = proj_ref[:, 4 * HG_WIDTH + 0 * RET_WIDTH:4 * HG_WIDTH + 1 * RET_WIDTH]
    rk = proj_ref[:, 4 * HG_WIDTH + 1 * RET_WIDTH:4 * HG_WIDTH + 2 * RET_WIDTH]
    rv = proj_ref[:, 4 * HG_WIDTH + 2 * RET_WIDTH:4 * HG_WIDTH + 3 * RET_WIDTH]
    rg = proj_ref[:, 4 * HG_WIDTH + 3 * RET_WIDTH:4 * HG_WIDTH + 4 * RET_WIDTH]
    cos = cos_ref[...]
    sin = sin_ref[...]
    j = row.astype(F32)
    rel = (t_idx - s_idx).astype(F32)
    r_gate = _silu(rg)

    for h in range(RET_HEADS):
        lo = h * RET_HEAD_DIM
        mid = lo + ROT_HALF
        hi = lo + RET_HEAD_DIM
        log_gamma = float(np.log(np.float32(1.0) - np.float32(2.0) ** np.float32(-5.0 - h)))
        q1, q2 = rq[:, lo:mid], rq[:, mid:hi]
        k1, k2 = rk[:, lo:mid], rk[:, mid:hi]
        qh = jnp.concatenate([q1 * cos - q2 * sin, q1 * sin + q2 * cos], axis=-1)
        kh = jnp.concatenate([k1 * cos - k2 * sin, k1 * sin + k2 * cos], axis=-1) * (RET_HEAD_DIM ** -0.5)
        if mask_pad:
            kh = jnp.where(valid, kh, 0.0)
        vh = rv[:, lo:hi]
        vh_bf = vh.astype(BF16)
        intra_decay = jnp.where(causal, jnp.exp(jnp.maximum(rel, 0.0) * log_gamma), 0.0)
        scores = lax.dot_general(qh.astype(BF16), kh.astype(BF16), _NT, preferred_element_type=F32)
        scores = (scores * intra_decay).astype(BF16)
        o = jnp.dot(scores, vh_bf, preferred_element_type=F32)
        k_dec = (kh * jnp.exp((CHUNK - 1.0 - j) * log_gamma)).astype(BF16)
        q_dec = (qh * jnp.exp((j + 1.0) * log_gamma)).astype(BF16)
        state_t = r_ref[h]
        o = o + lax.dot_general(q_dec, state_t.astype(BF16), _NT, preferred_element_type=F32)
        r_ref[h] = (state_t * float(np.exp(CHUNK * log_gamma))
                    + jnp.dot(vh.T.astype(BF16), k_dec, preferred_element_type=F32))
        mu = jnp.mean(o, axis=-1, keepdims=True)
        oc = o - mu
        var = jnp.mean(oc * oc, axis=-1, keepdims=True)
        o = oc * lax.rsqrt(var + HEAD_NORM_EPS) * rng_ref[:, lo:hi] + rnb_ref[:, lo:hi]
        out_ref[:, HG_WIDTH + lo:HG_WIDTH + hi] = (o * r_gate[:, lo:hi]).astype(out_ref.dtype)

    @pl.when(c == pl.num_programs(0) - 1)
    def _():
        s_out_ref[...] = s_ref[...]
        r_out_ref[...] = r_ref[...]


def _mixer(proj, cos, sin, hlb, hgn, rng, rnb, s_in, r_in, *, mask_pad):
    m = proj.shape[0]
    n_chunks = m // CHUNK
    vec = lambda w: pl.BlockSpec((1, w), lambda c: (0, 0))
    s_spec = pl.BlockSpec((HG_HEADS, HG_HEAD_DIM, HG_HEAD_DIM), lambda c: (0, 0, 0))
    r_spec = pl.BlockSpec((RET_HEADS, RET_HEAD_DIM, RET_HEAD_DIM), lambda c: (0, 0, 0))
    return pl.pallas_call(
        functools.partial(_mixer_kernel, mask_pad=mask_pad),
        grid=(n_chunks,),
        in_specs=[
            pl.BlockSpec((CHUNK, N_PROJ), lambda c: (c, 0)),
            pl.BlockSpec((CHUNK, ROT_HALF), lambda c: (c, 0)),
            pl.BlockSpec((CHUNK, ROT_HALF), lambda c: (c, 0)),
            pl.BlockSpec((2, HG_WIDTH), lambda c: (0, 0)),
            vec(HG_WIDTH), vec(RET_WIDTH), vec(RET_WIDTH),
            s_spec, r_spec,
        ],
        out_specs=[
            pl.BlockSpec((CHUNK, HG_WIDTH + RET_WIDTH), lambda c: (c, 0)),
            s_spec, r_spec,
        ],
        out_shape=[
            jax.ShapeDtypeStruct((m, HG_WIDTH + RET_WIDTH), BF16),
            jax.ShapeDtypeStruct(s_in.shape, F32),
            jax.ShapeDtypeStruct(r_in.shape, F32),
        ],
        scratch_shapes=[pltpu.VMEM(s_in.shape, F32), pltpu.VMEM(r_in.shape, F32)],
        compiler_params=pltpu.CompilerParams(
            dimension_semantics=("arbitrary",),
            vmem_limit_bytes=VMEM_LIMIT_BYTES),
        name="mixer_pad" if mask_pad else "mixer",
    )(proj, cos, sin, hlb, hgn, rng, rnb, s_in, r_in)


def _out_proj_kernel(mix_ref, x_ref, gin_ref, bin_ref, w_ref, g1_ref, b1_ref, hf_ref, hb_ref):
    h = _layer_norm(x_ref[...], gin_ref[...], bin_ref[...])
    mixed = jnp.dot(mix_ref[...], w_ref[...], preferred_element_type=F32)
    h1 = _layer_norm(DEEPNORM_ALPHA * h + mixed, g1_ref[...], b1_ref[...])
    hf_ref[...] = h1
    hb_ref[...] = h1.astype(BF16)


def _out_proj(mix, x, gin, bin_, w_bf16, g1, b1, *, tm):
    m = x.shape[0]
    vec = pl.BlockSpec((1, D_MODEL), lambda i: (0, 0))
    row = pl.BlockSpec((tm, D_MODEL), lambda i: (i, 0))
    return pl.pallas_call(
        _out_proj_kernel,
        grid=(m // tm,),
        in_specs=[row, row, vec, vec,
                  pl.BlockSpec((D_MODEL, D_MODEL), lambda i: (0, 0)), vec, vec],
        out_specs=[row, row],
        out_shape=[jax.ShapeDtypeStruct((m, D_MODEL), F32),
                   jax.ShapeDtypeStruct((m, D_MODEL), BF16)],
        compiler_params=pltpu.CompilerParams(
            dimension_semantics=("parallel",),
            vmem_limit_bytes=VMEM_LIMIT_BYTES),
        name="out_proj_ln1",
    )(mix, x, gin, bin_, w_bf16, g1, b1)


def _ffn_kernel(hb_ref, hf_ref, wg_ref, wu_ref, wd_ref, g2_ref, b2_ref, o_ref, acc_ref):
    k = pl.program_id(1)

    @pl.when(k == 0)
    def _():
        acc_ref[...] = jnp.zeros_like(acc_ref)

    hb = hb_ref[...]
    gate = jnp.dot(hb, wg_ref[...], preferred_element_type=F32)
    up = jnp.dot(hb, wu_ref[...], preferred_element_type=F32)
    act = (_silu(gate) * up).astype(BF16)
    acc_ref[...] += jnp.dot(act, wd_ref[...], preferred_element_type=F32)

    @pl.when(k == pl.num_programs(1) - 1)
    def _():
        o_ref[...] = _layer_norm(DEEPNORM_ALPHA * hf_ref[...] + acc_ref[...],
                                 g2_ref[...], b2_ref[...])


def _ffn(hb, hf, wg, wu, wd, g2, b2, *, tm, tf):
    m = hb.shape[0]
    vec = pl.BlockSpec((1, D_MODEL), lambda i, k: (0, 0))
    row = pl.BlockSpec((tm, D_MODEL), lambda i, k: (i, 0))
    return pl.pallas_call(
        _ffn_kernel,
        grid=(m // tm, D_FF // tf),
        in_specs=[row, row,
                  pl.BlockSpec((D_MODEL, tf), lambda i, k: (0, k)),
                  pl.BlockSpec((D_MODEL, tf), lambda i, k: (0, k)),
                  pl.BlockSpec((tf, D_MODEL), lambda i, k: (k, 0)),
                  vec, vec],
        out_specs=row,
        out_shape=jax.ShapeDtypeStruct((m, D_MODEL), F32),
        scratch_shapes=[pltpu.VMEM((tm, D_MODEL), F32)],
        compiler_params=pltpu.CompilerParams(
            dimension_semantics=("parallel", "arbitrary"),
            vmem_limit_bytes=VMEM_LIMIT_BYTES),
        name="ffn_ln2",
    )(hb, hf, wg, wu, wd, g2, b2)


def kernel(x, meta_tokens, ln_in_g, ln_in_b, hg_lower_bounds, w_in, hg_norm_g, ret_norm_g,
           ret_norm_b, w_out, ln1_g, ln1_b, w_gate, w_up, w_down, ln2_g, ln2_b):
    b_sz, seq, _ = x.shape
    assert b_sz == 1 and seq % CHUNK == 0 and DEPTH == 1
    x2 = x[0]
    row = lambda p: p.reshape(1, -1).astype(F32)
    gin, bin_ = row(ln_in_g), row(ln_in_b)

    length = PAD_FRONT + N_META + seq
    pos = (jnp.arange(length, dtype=jnp.int32) - PAD_FRONT).astype(F32)
    inv_freq = ROPE_BASE ** (-jnp.arange(ROT_HALF, dtype=F32) / ROT_HALF)
    ang = pos[:, None] * inv_freq[None, :]
    cos, sin = jnp.cos(ang), jnp.sin(ang)

    w_in_b = w_in[0].astype(BF16)
    w_out_b = w_out[0].astype(BF16)
    w_gate_b = w_gate[0].astype(BF16)
    w_up_b = w_up[0].astype(BF16)
    w_down_b = w_down[0].astype(BF16)
    hlb = hg_lower_bounds.astype(F32)
    hgn, rng, rnb = row(hg_norm_g[0]), row(ret_norm_g[0]), row(ret_norm_b[0])

    x_lead = jnp.concatenate([jnp.zeros((PAD_FRONT, D_MODEL), x.dtype),
                              meta_tokens.astype(x.dtype)], axis=0)
    proj_lead = _ln_proj(x_lead, gin, bin_, w_in_b, tm=CHUNK, tn=2048)
    s0 = jnp.zeros((HG_HEADS, HG_HEAD_DIM, HG_HEAD_DIM), F32)
    r0 = jnp.zeros((RET_HEADS, RET_HEAD_DIM, RET_HEAD_DIM), F32)
    _, s1, r1 = _mixer(proj_lead, cos[:CHUNK], sin[:CHUNK], hlb, hgn, rng, rnb, s0, r0,
                       mask_pad=True)

    proj = _ln_proj(x2, gin, bin_, w_in_b, tm=1024, tn=1024)
    mix, _, _ = _mixer(proj, cos[CHUNK:], sin[CHUNK:], hlb, hgn, rng, rnb, s1, r1,
                       mask_pad=False)
    h1f, h1b = _out_proj(mix, x2, gin, bin_, w_out_b, row(ln1_g[0]), row(ln1_b[0]), tm=256)
    out = _ffn(h1b, h1f, w_gate_b, w_up_b, w_down_b, row(ln2_g[0]), row(ln2_b[0]),
               tm=512, tf=512)
    return out[None]
```

```python
import functools

import numpy as np
import jax
import jax.numpy as jnp
from jax import lax
from jax.experimental import pallas as pl
from jax.experimental.pallas import tpu as pltpu

D_MODEL = 2048
CHUNK = 64
N_META = 16
PAD_FRONT = (-N_META) % CHUNK
HG_WIDTH = 1024
RET_WIDTH = 1024
GROUP = 1024
HG_HEAD_DIM = 128
HG_HEADS = HG_WIDTH // HG_HEAD_DIM
RET_HEADS = 4
RET_HEAD_DIM = RET_WIDTH // RET_HEADS
ROT_HALF = RET_HEAD_DIM // 2
D_FF = 5632
N_PROJ = 4 * HG_WIDTH + 4 * RET_WIDTH
ROPE_BASE = 10000.0
LN_EPS = 1e-5
HEAD_NORM_EPS = 1e-6
DEPTH = 1
DEEPNORM_ALPHA = (2.0 * DEPTH) ** 0.25

G_HQ, G_HF, G_HI, G_HG, G_RQ, G_RK, G_RV, G_RG = range(8)
PROJ_ORDER = (G_HQ, G_HI, G_HG, G_RQ, G_RK, G_RV, G_RG, G_HF)

F32 = jnp.float32
BF16 = jnp.bfloat16

VMEM_LIMIT_BYTES = 56 * 1024 * 1024
VMEM_LIMIT_FUSED_BYTES = 60 * 1024 * 1024

_NT = (((1,), (1,)), ((), ()))


def _layer_norm(x, g, b):
    mu = jnp.mean(x, axis=-1, keepdims=True)
    xc = x - mu
    var = jnp.mean(xc * xc, axis=-1, keepdims=True)
    return xc * lax.rsqrt(var + LN_EPS) * g + b


def _sigmoid(x):
    return 0.5 * jnp.tanh(0.5 * x) + 0.5


def _silu(x):
    return x * _sigmoid(x)


def _forget_lower_bound(hlb_ref):
    a0 = hlb_ref[0:1, :]
    a1 = hlb_ref[1:2, :]
    amax = jnp.maximum(a0, a1)
    e0 = jnp.exp(a0 - amax)
    e1 = jnp.exp(a1 - amax)
    return e0 / (e0 + e1)


def _chunk_consts():
    row = lax.broadcasted_iota(jnp.int32, (CHUNK, 1), 0)
    t_idx = lax.broadcasted_iota(jnp.int32, (CHUNK, CHUNK), 0)
    s_idx = lax.broadcasted_iota(jnp.int32, (CHUNK, CHUNK), 1)
    causal = t_idx >= s_idx
    j = row.astype(F32)
    rel = jnp.maximum(t_idx - s_idx, 0).astype(F32)
    log_gamma = [float(np.log(np.float32(1.0) - np.float32(2.0) ** np.float32(-5.0 - h)))
                 for h in range(RET_HEADS)]
    return dict(
        valid=row >= PAD_FRONT,
        causal=causal,
        tri=jnp.where(causal, 1.0, 0.0).astype(BF16),
        intra_decay=[jnp.where(causal, jnp.exp(rel * lg), 0.0) for lg in log_gamma],
        k_dec=[jnp.exp((CHUNK - 1.0 - j) * lg) for lg in log_gamma],
        q_dec=[jnp.exp((j + 1.0) * lg) for lg in log_gamma],
        chunk_gamma=[float(np.exp(CHUNK * lg)) for lg in log_gamma],
    )


SCAN_UNITS_PER_CHUNK = 28


def _mix_chunks_stages(chunks, lb, consts, hgn_ref, rng_ref, rnb_ref, s_ref, r_ref, out_ref,
                       mask_pad):
    causal = consts["causal"]
    valid = consts["valid"]
    hg_sl = [slice(h * HG_HEAD_DIM, (h + 1) * HG_HEAD_DIM) for h in range(HG_HEADS)]
    ret_sl = [slice(h * RET_HEAD_DIM, (h + 1) * RET_HEAD_DIM) for h in range(RET_HEADS)]

    for ck in chunks:
        f_pre = ck["cols"][G_HF].astype(F32)
        forget = lb + (1.0 - lb) * _sigmoid(f_pre)
        log_f = jnp.log(forget)
        kk = 1.0 - forget
        if mask_pad:
            log_f = jnp.where(valid, log_f, 0.0)
            kk = jnp.where(valid, kk, 0.0)
        lf_hi = log_f.astype(BF16)
        rem = log_f - lf_hi.astype(F32)
        lf_mid = rem.astype(BF16)
        lf_lo = (rem - lf_mid.astype(F32)).astype(BF16)
        ck["kk"] = kk
        ck["cum_terms"] = [jnp.dot(consts["tri"], t, preferred_element_type=F32)
                           for t in (lf_hi, lf_mid, lf_lo)]
        yield

    def advance_states(ck):
        ck["hg_inter"], ck["ret_inter"] = [], []
        for h, sl in enumerate(hg_sl):
            state_t = s_ref[h]
            ck["hg_inter"].append(lax.dot_general(ck["q_in"][:, sl], state_t.astype(BF16), _NT,
                                                  preferred_element_type=F32))
            s_ref[h] = state_t * ck["chunk_decay"][:, sl] + ck["hg_kv"][h]
            if h % 2 == 1:
                yield
        for h in range(RET_HEADS):
            state_t = r_ref[h]
            ck["ret_inter"].append(lax.dot_general(ck["q_dec"][h], state_t.astype(BF16), _NT,
                                                   preferred_element_type=F32))
            r_ref[h] = state_t * consts["chunk_gamma"][h] + ck["ret_kv"][h]
            yield

    for n, ck in enumerate(chunks):
        cols = ck["cols"]
        cum = ck["cum_terms"][0] + ck["cum_terms"][1] + ck["cum_terms"][2]
        ck["chunk_decay"] = jnp.exp(cum[CHUNK - 1:CHUNK, :])
        qs = _silu(cols[G_HQ].astype(F32)) * (HG_HEAD_DIM ** -0.5)
        ck["q_in"] = q_in = (qs * jnp.exp(cum)).astype(BF16)
        k_in = ck["kk"] * jnp.exp(-cum)
        k_out = (k_in * ck["chunk_decay"]).astype(BF16)
        k_in = k_in.astype(BF16)
        ck["v_bf"] = v_bf = cols[G_HI].astype(BF16)
        ck["hg_scores"], ck["hg_kv"] = [], []
        yield
        for h, sl in enumerate(hg_sl):
            ck["hg_scores"].append(lax.dot_general(q_in[:, sl], k_in[:, sl], _NT,
                                                   preferred_element_type=F32))
            v_t = v_bf[:, sl].astype(F32).T.astype(BF16)
            ck["hg_kv"].append(jnp.dot(v_t, k_out[:, sl], preferred_element_type=F32))
            if h % 2 == 1:
                yield

        rq = cols[G_RQ].astype(F32)
        rk = cols[G_RK].astype(F32)
        ck["rv_bf"] = rv_bf = cols[G_RV].astype(BF16)
        cos, sin = ck["cos"], ck["sin"]
        ck["ret_scores"], ck["ret_kv"], ck["q_dec"] = [], [], []
        for h, sl in enumerate(ret_sl):
            lo, mid, hi = sl.start, sl.start + ROT_HALF, sl.stop
            q1, q2 = rq[:, lo:mid], rq[:, mid:hi]
            k1, k2 = rk[:, lo:mid], rk[:, mid:hi]
            qh = jnp.concatenate([q1 * cos - q2 * sin, q1 * sin + q2 * cos], axis=-1)
            kh = (jnp.concatenate([k1 * cos - k2 * sin, k1 * sin + k2 * cos], axis=-1)
                  * (RET_HEAD_DIM ** -0.5))
            if mask_pad:
                kh = jnp.where(valid, kh, 0.0)
            ck["ret_scores"].append(lax.dot_general(qh.astype(BF16), kh.astype(BF16), _NT,
                                                    preferred_element_type=F32))
            k_dec = (kh * consts["k_dec"][h]).astype(BF16)
            ck["q_dec"].append((qh * consts["q_dec"][h]).astype(BF16))
            ck["ret_kv"].append(jnp.dot(rv_bf[:, sl].astype(F32).T.astype(BF16), k_dec,
                                        preferred_element_type=F32))
            yield
        if n == 0:
            yield from advance_states(ck)

    for ck in chunks:
        ck["hg_o"] = [jnp.dot(jnp.where(causal, ck["hg_scores"][h], 0.0).astype(BF16),
                              ck["v_bf"][:, sl], preferred_element_type=F32)
                      for h, sl in enumerate(hg_sl)]
        yield
        ck["ret_o"] = [jnp.dot((ck["ret_scores"][h] * consts["intra_decay"][h]).astype(BF16),
                               ck["rv_bf"][:, sl], preferred_element_type=F32)
                       for h, sl in enumerate(ret_sl)]
        yield
    for ck in chunks[1:]:
        yield from advance_states(ck)

    for ck in chunks:
        cols, rows = ck["cols"], ck["rows"]
        gate = _silu(cols[G_HG].astype(F32))
        for h, sl in enumerate(hg_sl):
            o = ck["hg_o"][h] + ck["hg_inter"][h]
            ms = jnp.mean(o * o, axis=-1, keepdims=True)
            o = o * lax.rsqrt(ms + HEAD_NORM_EPS) * hgn_ref[:, sl] * gate[:, sl]
            out_ref[rows, sl] = o.astype(out_ref.dtype)
            if h % 2 == 1:
                yield
        r_gate = _silu(cols[G_RG].astype(F32))
        for h, sl in enumerate(ret_sl):
            o = ck["ret_o"][h] + ck["ret_inter"][h]
            mu = jnp.mean(o, axis=-1, keepdims=True)
            oc = o - mu
            var = jnp.mean(oc * oc, axis=-1, keepdims=True)
            o = oc * lax.rsqrt(var + HEAD_NORM_EPS) * rng_ref[:, sl] + rnb_ref[:, sl]
            out_ref[rows, HG_WIDTH + sl.start:HG_WIDTH + sl.stop] = (
                o * r_gate[:, sl]).astype(out_ref.dtype)
            yield


def _ln_proj_kernel(x_ref, g_ref, b_ref, w_ref, o_ref, hb_ref):
    @pl.when(pl.program_id(1) == 0)
    def _():
        hb_ref[...] = _layer_norm(x_ref[...], g_ref[...], b_ref[...]).astype(BF16)

    o_ref[...] = jnp.dot(hb_ref[...], w_ref[...], preferred_element_type=F32)


def _ln_proj(x, g, b, w_bf16, *, tm, tn):
    m = x.shape[0]
    return pl.pallas_call(
        _ln_proj_kernel,
        grid=(m // tm, N_PROJ // tn),
        in_specs=[
            pl.BlockSpec((tm, D_MODEL), lambda i, j: (i, 0)),
            pl.BlockSpec((1, D_MODEL), lambda i, j: (0, 0)),
            pl.BlockSpec((1, D_MODEL), lambda i, j: (0, 0)),
            pl.BlockSpec((D_MODEL, tn), lambda i, j: (0, j)),
        ],
        out_specs=pl.BlockSpec((tm, tn), lambda i, j: (i, j)),
        out_shape=jax.ShapeDtypeStruct((m, N_PROJ), F32),
        scratch_shapes=[pltpu.VMEM((tm, D_MODEL), BF16)],
        compiler_params=pltpu.CompilerParams(
            dimension_semantics=("parallel", "arbitrary"),
            vmem_limit_bytes=VMEM_LIMIT_BYTES),
        name="ln_proj_lead",
    )(x, g, b, w_bf16)


def _lead_mixer_kernel(proj_ref, cos_ref, sin_ref, hlb_ref, hgn_ref, rng_ref, rnb_ref,
                       out_ref, s_out_ref, r_out_ref):
    s_out_ref[...] = jnp.zeros_like(s_out_ref)
    r_out_ref[...] = jnp.zeros_like(r_out_ref)
    chunk = dict(cols=[proj_ref[:, gi * GROUP:(gi + 1) * GROUP] for gi in range(8)],
                 cos=cos_ref[...], sin=sin_ref[...], rows=pl.ds(0, CHUNK))
    for _ in _mix_chunks_stages([chunk], _forget_lower_bound(hlb_ref), _chunk_consts(),
                                hgn_ref, rng_ref, rnb_ref, s_out_ref, r_out_ref, out_ref,
                                mask_pad=True):
        pass


def _lead_mixer(proj, cos, sin, hlb, hgn, rng, rnb):
    full = lambda shape: pl.BlockSpec(shape, lambda c: (0,) * len(shape))
    s_shape = (HG_HEADS, HG_HEAD_DIM, HG_HEAD_DIM)
    r_shape = (RET_HEADS, RET_HEAD_DIM, RET_HEAD_DIM)
    return pl.pallas_call(
        _lead_mixer_kernel,
        grid=(1,),
        in_specs=[full((CHUNK, N_PROJ)), full((CHUNK, ROT_HALF)), full((CHUNK, ROT_HALF)),
                  full((2, HG_WIDTH)), full((1, HG_WIDTH)), full((1, RET_WIDTH)),
                  full((1, RET_WIDTH))],
        out_specs=[full((CHUNK, HG_WIDTH + RET_WIDTH)), full(s_shape), full(r_shape)],
        out_shape=[jax.ShapeDtypeStruct((CHUNK, HG_WIDTH + RET_WIDTH), BF16),
                   jax.ShapeDtypeStruct(s_shape, F32),
                   jax.ShapeDtypeStruct(r_shape, F32)],
        compiler_params=pltpu.CompilerParams(
            dimension_semantics=("arbitrary",),
            vmem_limit_bytes=VMEM_LIMIT_BYTES),
        name="mixer_lead",
    )(proj, cos, sin, hlb, hgn, rng, rnb)


def _proj_mixer_kernel(x_ref, gin_ref, bin_ref, w_ref, cos_ref, sin_ref, hlb_ref, hgn_ref,
                       rng_ref, rnb_ref, s_in_ref, r_in_ref, out_ref,
                       hb_ref, acc_ref, pf0_ref, pf1_ref, pb0_ref, pb1_ref, s_ref, r_ref,
                       *, tm, groups_per_step):
    i = pl.program_id(0)
    j = pl.program_id(1)
    n_blocks = pl.num_programs(0) - 1
    n_j = N_PROJ // (groups_per_step * GROUP)
    chunks_per_step = tm // CHUNK // n_j

    @pl.when((i == 0) & (j == 0))
    def _():
        pf1_ref[...] = jnp.zeros_like(pf1_ref)
        pb1_ref[...] = jnp.zeros_like(pb1_ref)
        s_ref[...] = jnp.zeros_like(s_ref)
        r_ref[...] = jnp.zeros_like(r_ref)

    @pl.when((i == 1) & (j == 0))
    def _():
        s_ref[...] = s_in_ref[...]
        r_ref[...] = r_in_ref[...]

    @pl.when((j == 0) & (i < n_blocks))
    def _():
        for r0 in range(0, tm, CHUNK):
            rows = pl.ds(r0, CHUNK)
            hb_ref[rows, :] = _layer_norm(x_ref[rows, :], gin_ref[...], bin_ref[...]).astype(BF16)

    n_cols = k_cols = acc_ref.shape[1]

    def project_pieces(pf_ref, pb_ref):
        def make(g, p, k):
            def run():
                c0 = g * GROUP + p * n_cols
                part = jnp.dot(hb_ref[:, k * k_cols:(k + 1) * k_cols],
                               w_ref[k * k_cols:(k + 1) * k_cols, c0:c0 + n_cols],
                               preferred_element_type=F32)
                if k == 0:
                    acc_ref[...] = part
                elif k < D_MODEL // k_cols - 1:
                    acc_ref[...] += part
                else:
                    res = acc_ref[...] + part
                    pb_ref[j * groups_per_step + g, :, p * n_cols:(p + 1) * n_cols] = (
                        res.astype(BF16))
                    if g == groups_per_step - 1:
                        pf_ref[:, p * n_cols:(p + 1) * n_cols] = res
            return run
        return [make(g, p, k) for g in range(groups_per_step) for p in range(GROUP // n_cols)
                for k in range(D_MODEL // k_cols)]

    def scan_chunks(pf_ref, pb_ref):
        chunks = []
        for t in range(chunks_per_step):
            rows = pl.ds(pl.multiple_of((j * chunks_per_step + t) * CHUNK, CHUNK), CHUNK)
            cols = [None] * 8
            for pos, gi in enumerate(PROJ_ORDER):
                cols[gi] = pf_ref[rows, :] if gi == G_HF else pb_ref[pos, rows, :]
            chunks.append(dict(cols=cols, cos=cos_ref[rows, :], sin=sin_ref[rows, :], rows=rows))
        return _mix_chunks_stages(chunks, _forget_lower_bound(hlb_ref), _chunk_consts(),
                                  hgn_ref, rng_ref, rnb_ref, s_ref, r_ref, out_ref,
                                  mask_pad=False)

    def fused_step(wpf, wpb, rpf, rpb):
        units = SCAN_UNITS_PER_CHUNK * chunks_per_step
        pieces = project_pieces(wpf, wpb)
        scan = scan_chunks(rpf, rpb)
        done = 0
        for n, run_piece in enumerate(pieces):
            run_piece()
            target = (n + 1) * units // len(pieces)
            while done < target:
                next(scan, None)
                done += 1
        for _ in scan:
            pass

    even = i % 2 == 0

    @pl.when(even & (i < n_blocks))
    def _():
        fused_step(pf0_ref, pb0_ref, pf1_ref, pb1_ref)

    @pl.when(jnp.logical_not(even) & (i < n_blocks))
    def _():
        fused_step(pf1_ref, pb1_ref, pf0_ref, pb0_ref)

    @pl.when(i == n_blocks)
    def _():
        for _ in scan_chunks(pf1_ref, pb1_ref):
            pass


def _proj_mixer(x, gin, bin_, w_perm_bf16, cos, sin, hlb, hgn, rng, rnb, s_in, r_in, *, tm,
                groups_per_step, piece):
    m = x.shape[0]
    n_blocks = m // tm
    assert n_blocks % 2 == 0
    tn = groups_per_step * GROUP
    const = lambda shape: pl.BlockSpec(shape, lambda i, j: (0,) * len(shape))
    single = lambda shape: pl.BlockSpec(shape, lambda i, j: (0,) * len(shape),
                                        pipeline_mode=pl.Buffered(1))
    prev = lambda i, j: (jnp.maximum(i - 1, 0), 0)
    return pl.pallas_call(
        functools.partial(_proj_mixer_kernel, tm=tm, groups_per_step=groups_per_step),
        grid=(n_blocks + 1, N_PROJ // tn),
        in_specs=[
            pl.BlockSpec((tm, D_MODEL), lambda i, j: (jnp.minimum(i, n_blocks - 1), 0)),
            const((1, D_MODEL)), const((1, D_MODEL)),
            pl.BlockSpec((D_MODEL, tn), lambda i, j: (0, j)),
            pl.BlockSpec((tm, ROT_HALF), prev),
            pl.BlockSpec((tm, ROT_HALF), prev),
            const((2, HG_WIDTH)), const((1, HG_WIDTH)), const((1, RET_WIDTH)), const((1, RET_WIDTH)),
            single(s_in.shape), single(r_in.shape),
        ],
        out_specs=pl.BlockSpec((tm, HG_WIDTH + RET_WIDTH), prev),
        out_shape=jax.ShapeDtypeStruct((m, HG_WIDTH + RET_WIDTH), BF16),
        scratch_shapes=[
            pltpu.VMEM((tm, D_MODEL), BF16),
            pltpu.VMEM((tm, piece), F32),
            pltpu.VMEM((tm, GROUP), F32), pltpu.VMEM((tm, GROUP), F32),
            pltpu.VMEM((8, tm, GROUP), BF16), pltpu.VMEM((8, tm, GROUP), BF16),
            pltpu.VMEM(s_in.shape, F32),
            pltpu.VMEM(r_in.shape, F32),
        ],
        compiler_params=pltpu.CompilerParams(
            dimension_semantics=("arbitrary", "arbitrary"),
            vmem_limit_bytes=VMEM_LIMIT_FUSED_BYTES),
        name="proj_mixer",
    )(x, gin, bin_, w_perm_bf16, cos, sin, hlb, hgn, rng, rnb, s_in, r_in)


def _out_proj_kernel(mix_ref, x_ref, gin_ref, bin_ref, w_ref, g1_ref, b1_ref, hf_ref, hb_ref):
    h = _layer_norm(x_ref[...], gin_ref[...], bin_ref[...])
    mixed = jnp.dot(mix_ref[...], w_ref[...], preferred_element_type=F32)
    h1 = _layer_norm(DEEPNORM_ALPHA * h + mixed, g1_ref[...], b1_ref[...])
    hf_ref[...] = h1
    hb_ref[...] = h1.astype(BF16)


def _out_proj(mix, x, gin, bin_, w_bf16, g1, b1, *, tm):
    m = x.shape[0]
    vec = pl.BlockSpec((1, D_MODEL), lambda i: (0, 0))
    row = pl.BlockSpec((tm, D_MODEL), lambda i: (i, 0))
    return pl.pallas_call(
        _out_proj_kernel,
        grid=(m // tm,),
        in_specs=[row, row, vec, vec,
                  pl.BlockSpec((D_MODEL, D_MODEL), lambda i: (0, 0)), vec, vec],
        out_specs=[row, row],
        out_shape=[jax.ShapeDtypeStruct((m, D_MODEL), F32),
                   jax.ShapeDtypeStruct((m, D_MODEL), BF16)],
        compiler_params=pltpu.CompilerParams(
            dimension_semantics=("parallel",),
            vmem_limit_bytes=VMEM_LIMIT_BYTES),
        name="out_proj_ln1",
    )(mix, x, gin, bin_, w_bf16, g1, b1)


def _ffn_kernel(hb_ref, hf_ref, wg_ref, wu_ref, wd_ref, g2_ref, b2_ref, o_ref, acc_ref):
    k = pl.program_id(1)

    @pl.when(k == 0)
    def _():
        acc_ref[...] = jnp.zeros_like(acc_ref)

    hb = hb_ref[...]
    gate = jnp.dot(hb, wg_ref[...], preferred_element_type=F32)
    up = jnp.dot(hb, wu_ref[...], preferred_element_type=F32)
    act = (_silu(gate) * up).astype(BF16)
    acc_ref[...] += jnp.dot(act, wd_ref[...], preferred_element_type=F32)

    @pl.when(k == pl.num_programs(1) - 1)
    def _():
        o_ref[...] = _layer_norm(DEEPNORM_ALPHA * hf_ref[...] + acc_ref[...],
                                 g2_ref[...], b2_ref[...])


def _ffn(hb, hf, wg, wu, wd, g2, b2, *, tm, tf):
    m = hb.shape[0]
    vec = pl.BlockSpec((1, D_MODEL), lambda i, k: (0, 0))
    row = pl.BlockSpec((tm, D_MODEL), lambda i, k: (i, 0))
    return pl.pallas_call(
        _ffn_kernel,
        grid=(m // tm, D_FF // tf),
        in_specs=[row, row,
                  pl.BlockSpec((D_MODEL, tf), lambda i, k: (0, k)),
                  pl.BlockSpec((D_MODEL, tf), lambda i, k: (0, k)),
                  pl.BlockSpec((tf, D_MODEL), lambda i, k: (k, 0)),
                  vec, vec],
        out_specs=row,
        out_shape=jax.ShapeDtypeStruct((m, D_MODEL), F32),
        scratch_shapes=[pltpu.VMEM((tm, D_MODEL), F32)],
        compiler_params=pltpu.CompilerParams(
            dimension_semantics=("parallel", "arbitrary"),
            vmem_limit_bytes=VMEM_LIMIT_BYTES),
        name="ffn_ln2",
    )(hb, hf, wg, wu, wd, g2, b2)


def kernel(x, meta_tokens, ln_in_g, ln_in_b, hg_lower_bounds, w_in, hg_norm_g, ret_norm_g,
           ret_norm_b, w_out, ln1_g, ln1_b, w_gate, w_up, w_down, ln2_g, ln2_b):
    b_sz, seq, _ = x.shape
    assert b_sz == 1 and seq % CHUNK == 0 and DEPTH == 1
    x2 = x[0]
    row = lambda p: p.reshape(1, -1).astype(F32)
    gin, bin_ = row(ln_in_g), row(ln_in_b)

    length = PAD_FRONT + N_META + seq
    pos = (jnp.arange(length, dtype=jnp.int32) - PAD_FRONT).astype(F32)
    inv_freq = ROPE_BASE ** (-jnp.arange(ROT_HALF, dtype=F32) / ROT_HALF)
    ang = pos[:, None] * inv_freq[None, :]
    cos, sin = jnp.cos(ang), jnp.sin(ang)

    w_in_b = w_in[0].astype(BF16)
    w_out_b = w_out[0].astype(BF16)
    w_gate_b = w_gate[0].astype(BF16)
    w_up_b = w_up[0].astype(BF16)
    w_down_b = w_down[0].astype(BF16)
    hlb = hg_lower_bounds.astype(F32)
    hgn, rng, rnb = row(hg_norm_g[0]), row(ret_norm_g[0]), row(ret_norm_b[0])

    x_lead = jnp.concatenate([jnp.zeros((PAD_FRONT, D_MODEL), x.dtype),
                              meta_tokens.astype(x.dtype)], axis=0)
    proj_lead = _ln_proj(x_lead, gin, bin_, w_in_b, tm=CHUNK, tn=2048)
    _, s1, r1 = _lead_mixer(proj_lead, cos[:CHUNK], sin[:CHUNK], hlb, hgn, rng, rnb)

    w_in_perm = jnp.concatenate([w_in_b[:, gi * GROUP:(gi + 1) * GROUP] for gi in PROJ_ORDER],
                                axis=1)
    mix = _proj_mixer(x2, gin, bin_, w_in_perm, cos[CHUNK:], sin[CHUNK:], hlb, hgn, rng, rnb,
                      s1, r1, tm=512, groups_per_step=2, piece=512)
    h1f, h1b = _out_proj(mix, x2, gin, bin_, w_out_b, row(ln1_g[0]), row(ln1_b[0]), tm=256)
    out = _ffn(h1b, h1f, w_gate_b, w_up_b, w_down_b, row(ln2_g[0]), row(ln2_b[0]),
               tm=512, tf=512)
    return out[None]
```

```python
import functools

import numpy as np
import jax
import jax.numpy as jnp
from jax import lax
from jax.experimental import pallas as pl
from jax.experimental.pallas import tpu as pltpu

D_MODEL = 2048
CHUNK = 64
N_META = 16
PAD_FRONT = (-N_META) % CHUNK
HG_WIDTH = 1024
RET_WIDTH = 1024
GROUP = 1024
HG_HEAD_DIM = 128
HG_HEADS = HG_WIDTH // HG_HEAD_DIM
RET_HEADS = 4
RET_HEAD_DIM = RET_WIDTH // RET_HEADS
ROT_HALF = RET_HEAD_DIM // 2
D_FF = 5632
N_PROJ = 4 * HG_WIDTH + 4 * RET_WIDTH
ROPE_BASE = 10000.0
LN_EPS = 1e-5
HEAD_NORM_EPS = 1e-6
DEPTH = 1
DEEPNORM_ALPHA = (2.0 * DEPTH) ** 0.25

G_HQ, G_HF, G_HI, G_HG, G_RQ, G_RK, G_RV, G_RG = range(8)
GROUPS_PER_STEP = 2
PROJ_ORDER = tuple(range(GROUPS_PER_STEP, 8)) + tuple(range(GROUPS_PER_STEP))
assert PROJ_ORDER[-1] == G_HF

F32 = jnp.float32
BF16 = jnp.bfloat16

VMEM_LIMIT_BYTES = 56 * 1024 * 1024
VMEM_LIMIT_FUSED_BYTES = 60 * 1024 * 1024

_NT = (((1,), (1,)), ((), ()))


def _layer_norm(x, g, b):
    mu = jnp.mean(x, axis=-1, keepdims=True)
    xc = x - mu
    var = jnp.mean(xc * xc, axis=-1, keepdims=True)
    return xc * lax.rsqrt(var + LN_EPS) * g + b


def _sigmoid(x):
    return 0.5 * jnp.tanh(0.5 * x) + 0.5


def _silu(x):
    return x * _sigmoid(x)


def _forget_lower_bound(hlb_ref):
    a0 = hlb_ref[0:1, :]
    a1 = hlb_ref[1:2, :]
    amax = jnp.maximum(a0, a1)
    e0 = jnp.exp(a0 - amax)
    e1 = jnp.exp(a1 - amax)
    return e0 / (e0 + e1)


def _chunk_consts():
    row = lax.broadcasted_iota(jnp.int32, (CHUNK, 1), 0)
    t_idx = lax.broadcasted_iota(jnp.int32, (CHUNK, CHUNK), 0)
    s_idx = lax.broadcasted_iota(jnp.int32, (CHUNK, CHUNK), 1)
    causal = t_idx >= s_idx
    j = row.astype(F32)
    rel = jnp.maximum(t_idx - s_idx, 0).astype(F32)
    log_gamma = [float(np.log(np.float32(1.0) - np.float32(2.0) ** np.float32(-5.0 - h)))
                 for h in range(RET_HEADS)]
    return dict(
        valid=row >= PAD_FRONT,
        causal=causal,
        tri=jnp.where(causal, 1.0, 0.0).astype(BF16),
        intra_decay=[jnp.where(causal, jnp.exp(rel * lg), 0.0) for lg in log_gamma],
        k_dec=[jnp.exp((CHUNK - 1.0 - j) * lg) for lg in log_gamma],
        q_dec=[jnp.exp((j + 1.0) * lg) for lg in log_gamma],
        chunk_gamma=[float(np.exp(CHUNK * lg)) for lg in log_gamma],
    )


SCAN_UNITS_PER_CHUNK = 28


def _mix_chunks_stages(chunks, lb, consts, hgn_ref, rng_ref, rnb_ref, s_ref, r_ref, out_ref,
                       mask_pad):
    causal = consts["causal"]
    valid = consts["valid"]
    hg_sl = [slice(h * HG_HEAD_DIM, (h + 1) * HG_HEAD_DIM) for h in range(HG_HEADS)]
    ret_sl = [slice(h * RET_HEAD_DIM, (h + 1) * RET_HEAD_DIM) for h in range(RET_HEADS)]

    for ck in chunks:
        f_pre = ck["cols"][G_HF].astype(F32)
        forget = lb + (1.0 - lb) * _sigmoid(f_pre)
        log_f = jnp.log(forget)
        kk = 1.0 - forget
        if mask_pad:
            log_f = jnp.where(valid, log_f, 0.0)
            kk = jnp.where(valid, kk, 0.0)
        lf_hi = log_f.astype(BF16)
        rem = log_f - lf_hi.astype(F32)
        lf_mid = rem.astype(BF16)
        lf_lo = (rem - lf_mid.astype(F32)).astype(BF16)
        ck["kk"] = kk
        ck["cum_terms"] = [jnp.dot(consts["tri"], t, preferred_element_type=F32)
                           for t in (lf_hi, lf_mid, lf_lo)]
        yield

    def advance_states(ck):
        ck["hg_inter"], ck["ret_inter"] = [], []
        for h, sl in enumerate(hg_sl):
            state_t = s_ref[h]
            ck["hg_inter"].append(lax.dot_general(ck["q_in"][:, sl], state_t.astype(BF16), _NT,
                                                  preferred_element_type=F32))
            s_ref[h] = state_t * ck["chunk_decay"][:, sl] + ck["hg_kv"][h]
            if h % 2 == 1:
                yield
        for h in range(RET_HEADS):
            state_t = r_ref[h]
            ck["ret_inter"].append(lax.dot_general(ck["q_dec"][h], state_t.astype(BF16), _NT,
                                                   preferred_element_type=F32))
            r_ref[h] = state_t * consts["chunk_gamma"][h] + ck["ret_kv"][h]
            yield

    for n, ck in enumerate(chunks):
        cols = ck["cols"]
        cum = ck["cum_terms"][0] + ck["cum_terms"][1] + ck["cum_terms"][2]
        ck["chunk_decay"] = jnp.exp(cum[CHUNK - 1:CHUNK, :])
        qs = _silu(cols[G_HQ].astype(F32)) * (HG_HEAD_DIM ** -0.5)
        ck["q_in"] = q_in = (qs * jnp.exp(cum)).astype(BF16)
        k_in = ck["kk"] * jnp.exp(-cum)
        k_out = (k_in * ck["chunk_decay"]).astype(BF16)
        k_in = k_in.astype(BF16)
        ck["v_bf"] = v_bf = cols[G_HI].astype(BF16)
        ck["hg_scores"], ck["hg_kv"] = [], []
        yield
        for h, sl in enumerate(hg_sl):
            ck["hg_scores"].append(lax.dot_general(q_in[:, sl], k_in[:, sl], _NT,
                                                   preferred_element_type=F32))
            v_t = v_bf[:, sl].astype(F32).T.astype(BF16)
            ck["hg_kv"].append(jnp.dot(v_t, k_out[:, sl], preferred_element_type=F32))
            if h % 2 == 1:
                yield

        rq = cols[G_RQ].astype(F32)
        rk = cols[G_RK].astype(F32)
        ck["rv_bf"] = rv_bf = cols[G_RV].astype(BF16)
        cos, sin = ck["cos"], ck["sin"]
        ck["ret_scores"], ck["ret_kv"], ck["q_dec"] = [], [], []
        for h, sl in enumerate(ret_sl):
            lo, mid, hi = sl.start, sl.start + ROT_HALF, sl.stop
            q1, q2 = rq[:, lo:mid], rq[:, mid:hi]
            k1, k2 = rk[:, lo:mid], rk[:, mid:hi]
            qh = jnp.concatenate([q1 * cos - q2 * sin, q1 * sin + q2 * cos], axis=-1)
            kh = (jnp.concatenate([k1 * cos - k2 * sin, k1 * sin + k2 * cos], axis=-1)
                  * (RET_HEAD_DIM ** -0.5))
            if mask_pad:
                kh = jnp.where(valid, kh, 0.0)
            ck["ret_scores"].append(lax.dot_general(qh.astype(BF16), kh.astype(BF16), _NT,
                                                    preferred_element_type=F32))
            k_dec = (kh * consts["k_dec"][h]).astype(BF16)
            ck["q_dec"].append((qh * consts["q_dec"][h]).astype(BF16))
            ck["ret_kv"].append(jnp.dot(rv_bf[:, sl].astype(F32).T.astype(BF16), k_dec,
                                        preferred_element_type=F32))
            yield
        if n == 0:
            yield from advance_states(ck)

    for ck in chunks:
        ck["hg_o"] = [jnp.dot(jnp.where(causal, ck["hg_scores"][h], 0.0).astype(BF16),
                              ck["v_bf"][:, sl], preferred_element_type=F32)
                      for h, sl in enumerate(hg_sl)]
        yield
        ck["ret_o"] = [jnp.dot((ck["ret_scores"][h] * consts["intra_decay"][h]).astype(BF16),
                               ck["rv_bf"][:, sl], preferred_element_type=F32)
                       for h, sl in enumerate(ret_sl)]
        yield
    for ck in chunks[1:]:
        yield from advance_states(ck)

    for ck in chunks:
        cols, rows = ck["cols"], ck["rows"]
        gate = _silu(cols[G_HG].astype(F32))
        for h, sl in enumerate(hg_sl):
            o = ck["hg_o"][h] + ck["hg_inter"][h]
            ms = jnp.mean(o * o, axis=-1, keepdims=True)
            o = o * lax.rsqrt(ms + HEAD_NORM_EPS) * hgn_ref[:, sl] * gate[:, sl]
            out_ref[rows, sl] = o.astype(out_ref.dtype)
            if h % 2 == 1:
                yield
        r_gate = _silu(cols[G_RG].astype(F32))
        for h, sl in enumerate(ret_sl):
            o = ck["ret_o"][h] + ck["ret_inter"][h]
            mu = jnp.mean(o, axis=-1, keepdims=True)
            oc = o - mu
            var = jnp.mean(oc * oc, axis=-1, keepdims=True)
            o = oc * lax.rsqrt(var + HEAD_NORM_EPS) * rng_ref[:, sl] + rnb_ref[:, sl]
            out_ref[rows, HG_WIDTH + sl.start:HG_WIDTH + sl.stop] = (
                o * r_gate[:, sl]).astype(out_ref.dtype)
            yield


def _ln_proj_kernel(x_ref, g_ref, b_ref, w_ref, o_ref, hb_ref):
    @pl.when(pl.program_id(1) == 0)
    def _():
        hb_ref[...] = _layer_norm(x_ref[...], g_ref[...], b_ref[...]).astype(BF16)

    o_ref[...] = jnp.dot(hb_ref[...], w_ref[...], preferred_element_type=F32)


def _ln_proj(x, g, b, w_bf16, *, tm, tn):
    m = x.shape[0]
    return pl.pallas_call(
        _ln_proj_kernel,
        grid=(m // tm, N_PROJ // tn),
        in_specs=[
            pl.BlockSpec((tm, D_MODEL), lambda i, j: (i, 0)),
            pl.BlockSpec((1, D_MODEL), lambda i, j: (0, 0)),
            pl.BlockSpec((1, D_MODEL), lambda i, j: (0, 0)),
            pl.BlockSpec((D_MODEL, tn), lambda i, j: (0, j)),
        ],
        out_specs=pl.BlockSpec((tm, tn), lambda i, j: (i, j)),
        out_shape=jax.ShapeDtypeStruct((m, N_PROJ), F32),
        scratch_shapes=[pltpu.VMEM((tm, D_MODEL), BF16)],
        compiler_params=pltpu.CompilerParams(
            dimension_semantics=("parallel", "arbitrary"),
            vmem_limit_bytes=VMEM_LIMIT_BYTES),
        name="ln_proj_lead",
    )(x, g, b, w_bf16)


def _lead_mixer_kernel(proj_ref, cos_ref, sin_ref, hlb_ref, hgn_ref, rng_ref, rnb_ref,
                       out_ref, s_out_ref, r_out_ref):
    s_out_ref[...] = jnp.zeros_like(s_out_ref)
    r_out_ref[...] = jnp.zeros_like(r_out_ref)
    chunk = dict(cols=[proj_ref[:, gi * GROUP:(gi + 1) * GROUP] for gi in range(8)],
                 cos=cos_ref[...], sin=sin_ref[...], rows=pl.ds(0, CHUNK))
    for _ in _mix_chunks_stages([chunk], _forget_lower_bound(hlb_ref), _chunk_consts(),
                                hgn_ref, rng_ref, rnb_ref, s_out_ref, r_out_ref, out_ref,
                                mask_pad=True):
        pass


def _lead_mixer(proj, cos, sin, hlb, hgn, rng, rnb):
    full = lambda shape: pl.BlockSpec(shape, lambda c: (0,) * len(shape))
    s_shape = (HG_HEADS, HG_HEAD_DIM, HG_HEAD_DIM)
    r_shape = (RET_HEADS, RET_HEAD_DIM, RET_HEAD_DIM)
    return pl.pallas_call(
        _lead_mixer_kernel,
        grid=(1,),
        in_specs=[full((CHUNK, N_PROJ)), full((CHUNK, ROT_HALF)), full((CHUNK, ROT_HALF)),
                  full((2, HG_WIDTH)), full((1, HG_WIDTH)), full((1, RET_WIDTH)),
                  full((1, RET_WIDTH))],
        out_specs=[full((CHUNK, HG_WIDTH + RET_WIDTH)), full(s_shape), full(r_shape)],
        out_shape=[jax.ShapeDtypeStruct((CHUNK, HG_WIDTH + RET_WIDTH), BF16),
                   jax.ShapeDtypeStruct(s_shape, F32),
                   jax.ShapeDtypeStruct(r_shape, F32)],
        compiler_params=pltpu.CompilerParams(
            dimension_semantics=("arbitrary",),
            vmem_limit_bytes=VMEM_LIMIT_BYTES),
        name="mixer_lead",
    )(proj, cos, sin, hlb, hgn, rng, rnb)


def _proj_mixer_kernel(x_ref, gin_ref, bin_ref, w_ref, cos_ref, sin_ref, hlb_ref, hgn_ref,
                       rng_ref, rnb_ref, s_in_ref, r_in_ref, out_ref,
                       hb_ref, acc_ref, pf0_ref, pf1_ref, pb0_ref, pb1_ref, s_ref, r_ref,
                       *, tm):
    i = pl.program_id(0)
    j = pl.program_id(1)
    n_blocks = pl.num_programs(0) - 1
    groups_per_step = GROUPS_PER_STEP
    n_j = N_PROJ // (groups_per_step * GROUP)
    chunks_per_step = tm // CHUNK // n_j

    @pl.when((i == 0) & (j == 0))
    def _():
        pf1_ref[...] = jnp.zeros_like(pf1_ref)
        pb1_ref[...] = jnp.zeros_like(pb1_ref)
        s_ref[...] = jnp.zeros_like(s_ref)
        r_ref[...] = jnp.zeros_like(r_ref)

    @pl.when((i == 1) & (j == 0))
    def _():
        s_ref[...] = s_in_ref[...]
        r_ref[...] = r_in_ref[...]

    @pl.when((j == 0) & (i < n_blocks))
    def _():
        for r0 in range(0, tm, CHUNK):
            rows = pl.ds(r0, CHUNK)
            hb_ref[rows, :] = _layer_norm(x_ref[rows, :], gin_ref[...], bin_ref[...]).astype(BF16)

    n_cols = k_cols = acc_ref.shape[1]

    def project_pieces(pf_ref, pb_ref):
        def make(g, p, k):
            def run():
                c0 = g * GROUP + p * n_cols
                part = jnp.dot(hb_ref[:, k * k_cols:(k + 1) * k_cols],
                               w_ref[k * k_cols:(k + 1) * k_cols, c0:c0 + n_cols],
                               preferred_element_type=F32)
                if k == 0:
                    acc_ref[...] = part
                elif k < D_MODEL // k_cols - 1:
                    acc_ref[...] += part
                else:
                    res = acc_ref[...] + part
                    pb_ref[j * groups_per_step + g, :, p * n_cols:(p + 1) * n_cols] = (
                        res.astype(BF16))
                    if g == groups_per_step - 1:
                        pf_ref[:, p * n_cols:(p + 1) * n_cols] = res
            return run
        return [make(g, p, k) for g in range(groups_per_step) for p in range(GROUP // n_cols)
                for k in range(D_MODEL // k_cols)]

    def scan_chunks(pf_ref, pb_ref):
        chunks = []
        for t in range(chunks_per_step):
            rows = pl.ds(pl.multiple_of((j * chunks_per_step + t) * CHUNK, CHUNK), CHUNK)
            cols = [None] * 8
            for pos, gi in enumerate(PROJ_ORDER):
                cols[gi] = pf_ref[rows, :] if gi == G_HF else pb_ref[pos, rows, :]
            chunks.append(dict(cols=cols, cos=cos_ref[rows, :], sin=sin_ref[rows, :], rows=rows))
        return _mix_chunks_stages(chunks, _forget_lower_bound(hlb_ref), _chunk_consts(),
                                  hgn_ref, rng_ref, rnb_ref, s_ref, r_ref, out_ref,
                                  mask_pad=False)

    def fused_step(wpf, wpb, rpf, rpb):
        units = SCAN_UNITS_PER_CHUNK * chunks_per_step
        pieces = project_pieces(wpf, wpb)
        scan = scan_chunks(rpf, rpb)
        done = 0
        for n, run_piece in enumerate(pieces):
            run_piece()
            target = (n + 1) * units // len(pieces)
            while done < target:
                next(scan, None)
                done += 1
        for _ in scan:
            pass

    even = i % 2 == 0

    @pl.when(even & (i < n_blocks))
    def _():
        fused_step(pf0_ref, pb0_ref, pf1_ref, pb1_ref)

    @pl.when(jnp.logical_not(even) & (i < n_blocks))
    def _():
        fused_step(pf1_ref, pb1_ref, pf0_ref, pb0_ref)

    @pl.when(i == n_blocks)
    def _():
        for _ in scan_chunks(pf1_ref, pb1_ref):
            pass


def _proj_mixer(x, gin, bin_, w_bf16, cos, sin, hlb, hgn, rng, rnb, s_in, r_in, *, tm, piece):
    m = x.shape[0]
    n_blocks = m // tm
    assert n_blocks % 2 == 0
    tn = GROUPS_PER_STEP * GROUP
    n_j = N_PROJ // tn
    const = lambda shape: pl.BlockSpec(shape, lambda i, j: (0,) * len(shape))
    single = lambda shape: pl.BlockSpec(shape, lambda i, j: (0,) * len(shape),
                                        pipeline_mode=pl.Buffered(1))
    prev = lambda i, j: (jnp.maximum(i - 1, 0), 0)
    return pl.pallas_call(
        functools.partial(_proj_mixer_kernel, tm=tm),
        grid=(n_blocks + 1, n_j),
        in_specs=[
            pl.BlockSpec((tm, D_MODEL), lambda i, j: (jnp.minimum(i, n_blocks - 1), 0)),
            const((1, D_MODEL)), const((1, D_MODEL)),
            pl.BlockSpec((D_MODEL, tn), lambda i, j: (0, (j + 1) % n_j)),
            pl.BlockSpec((tm, ROT_HALF), prev),
            pl.BlockSpec((tm, ROT_HALF), prev),
            const((2, HG_WIDTH)), const((1, HG_WIDTH)), const((1, RET_WIDTH)), const((1, RET_WIDTH)),
            single(s_in.shape), single(r_in.shape),
        ],
        out_specs=pl.BlockSpec((tm, HG_WIDTH + RET_WIDTH), prev),
        out_shape=jax.ShapeDtypeStruct((m, HG_WIDTH + RET_WIDTH), BF16),
        scratch_shapes=[
            pltpu.VMEM((tm, D_MODEL), BF16),
            pltpu.VMEM((tm, piece), F32),
            pltpu.VMEM((tm, GROUP), F32), pltpu.VMEM((tm, GROUP), F32),
            pltpu.VMEM((8, tm, GROUP), BF16), pltpu.VMEM((8, tm, GROUP), BF16),
            pltpu.VMEM(s_in.shape, F32),
            pltpu.VMEM(r_in.shape, F32),
        ],
        compiler_params=pltpu.CompilerParams(
            dimension_semantics=("arbitrary", "arbitrary"),
            vmem_limit_bytes=VMEM_LIMIT_FUSED_BYTES),
        name="proj_mixer",
    )(x, gin, bin_, w_bf16, cos, sin, hlb, hgn, rng, rnb, s_in, r_in)


def _out_proj_kernel(mix_ref, x_ref, gin_ref, bin_ref, w_ref, g1_ref, b1_ref, hb_ref):
    h = _layer_norm(x_ref[...], gin_ref[...], bin_ref[...])
    mixed = jnp.dot(mix_ref[...], w_ref[...], preferred_element_type=F32)
    h1 = _layer_norm(DEEPNORM_ALPHA * h + mixed, g1_ref[...], b1_ref[...])
    hb_ref[...] = h1.astype(BF16)


def _out_proj(mix, x, gin, bin_, w_bf16, g1, b1, *, tm):
    m = x.shape[0]
    vec = pl.BlockSpec((1, D_MODEL), lambda i: (0, 0))
    row = pl.BlockSpec((tm, D_MODEL), lambda i: (i, 0))
    return pl.pallas_call(
        _out_proj_kernel,
        grid=(m // tm,),
        in_specs=[row, row, vec, vec,
                  pl.BlockSpec((D_MODEL, D_MODEL), lambda i: (0, 0)), vec, vec],
        out_specs=row,
        out_shape=jax.ShapeDtypeStruct((m, D_MODEL), BF16),
        compiler_params=pltpu.CompilerParams(
            dimension_semantics=("parallel",),
            vmem_limit_bytes=VMEM_LIMIT_BYTES),
        name="out_proj_ln1",
    )(mix, x, gin, bin_, w_bf16, g1, b1)


def _ffn_kernel(hb_ref, wg_ref, wu_ref, wd_ref, g2_ref, b2_ref, o_ref):
    k = pl.program_id(1)

    @pl.when(k == 0)
    def _():
        o_ref[...] = jnp.zeros_like(o_ref)

    hb = hb_ref[...]
    gate = jnp.dot(hb, wg_ref[...], preferred_element_type=F32)
    up = jnp.dot(hb, wu_ref[...], preferred_element_type=F32)
    act = (_silu(gate) * up).astype(BF16)
    o_ref[...] += jnp.dot(act, wd_ref[...], preferred_element_type=F32)

    @pl.when(k == pl.num_programs(1) - 1)
    def _():
        for r0 in range(0, o_ref.shape[0], 128):
            rows = pl.ds(r0, 128)
            o_ref[rows, :] = _layer_norm(
                DEEPNORM_ALPHA * hb_ref[rows, :].astype(F32) + o_ref[rows, :],
                g2_ref[...], b2_ref[...])


def _ffn(hb, wg, wu, wd, g2, b2, *, tm, tf):
    m = hb.shape[0]
    vec = pl.BlockSpec((1, D_MODEL), lambda i, k: (0, 0))
    row = pl.BlockSpec((tm, D_MODEL), lambda i, k: (i, 0))
    return pl.pallas_call(
        _ffn_kernel,
        grid=(m // tm, D_FF // tf),
        in_specs=[row,
                  pl.BlockSpec((D_MODEL, tf), lambda i, k: (0, k)),
                  pl.BlockSpec((D_MODEL, tf), lambda i, k: (0, k)),
                  pl.BlockSpec((tf, D_MODEL), lambda i, k: (k, 0)),
                  vec, vec],
        out_specs=row,
        out_shape=jax.ShapeDtypeStruct((m, D_MODEL), F32),
        compiler_params=pltpu.CompilerParams(
            dimension_semantics=("parallel", "arbitrary"),
            vmem_limit_bytes=VMEM_LIMIT_BYTES),
        name="ffn_ln2",
    )(hb, wg, wu, wd, g2, b2)


def kernel(x, meta_tokens, ln_in_g, ln_in_b, hg_lower_bounds, w_in, hg_norm_g, ret_norm_g,
           ret_norm_b, w_out, ln1_g, ln1_b, w_gate, w_up, w_down, ln2_g, ln2_b):
    b_sz, seq, _ = x.shape
    assert b_sz == 1 and seq % CHUNK == 0 and DEPTH == 1
    x2 = x[0]
    row = lambda p: p.reshape(1, -1).astype(F32)
    gin, bin_ = row(ln_in_g), row(ln_in_b)

    length = PAD_FRONT + N_META + seq
    pos = (jnp.arange(length, dtype=jnp.int32) - PAD_FRONT).astype(F32)
    inv_freq = ROPE_BASE ** (-jnp.arange(ROT_HALF, dtype=F32) / ROT_HALF)
    ang = pos[:, None] * inv_freq[None, :]
    cos, sin = jnp.cos(ang), jnp.sin(ang)

    w_in_b = w_in[0].astype(BF16)
    w_out_b = w_out[0].astype(BF16)
    w_gate_b = w_gate[0].astype(BF16)
    w_up_b = w_up[0].astype(BF16)
    w_down_b = w_down[0].astype(BF16)
    hlb = hg_lower_bounds.astype(F32)
    hgn, rng, rnb = row(hg_norm_g[0]), row(ret_norm_g[0]), row(ret_norm_b[0])

    x_lead = jnp.concatenate([jnp.zeros((PAD_FRONT, D_MODEL), x.dtype),
                              meta_tokens.astype(x.dtype)], axis=0)
    proj_lead = _ln_proj(x_lead, gin, bin_, w_in_b, tm=CHUNK, tn=2048)
    _, s1, r1 = _lead_mixer(proj_lead, cos[:CHUNK], sin[:CHUNK], hlb, hgn, rng, rnb)

    mix = _proj_mixer(x2, gin, bin_, w_in_b, cos[CHUNK:], sin[CHUNK:], hlb, hgn, rng, rnb,
                      s1, r1, tm=512, piece=512)
    h1b = _out_proj(mix, x2, gin, bin_, w_out_b, row(ln1_g[0]), row(ln1_b[0]), tm=256)
    out = _ffn(h1b, w_gate_b, w_up_b, w_down_b, row(ln2_g[0]), row(ln2_b[0]), tm=1024, tf=512)
    return out[None]
```

```python
import functools

import numpy as np
import jax
import jax.numpy as jnp
from jax import lax
from jax.experimental import pallas as pl
from jax.experimental.pallas import tpu as pltpu

D_MODEL = 2048
CHUNK = 64
N_META = 16
PAD_FRONT = (-N_META) % CHUNK
HG_WIDTH = 1024
RET_WIDTH = 1024
GROUP = 1024
HG_HEAD_DIM = 128
HG_HEADS = HG_WIDTH // HG_HEAD_DIM
RET_HEADS = 4
RET_HEAD_DIM = RET_WIDTH // RET_HEADS
ROT_HALF = RET_HEAD_DIM // 2
D_FF = 5632
N_PROJ = 4 * HG_WIDTH + 4 * RET_WIDTH
ROPE_BASE = 10000.0
LN_EPS = 1e-5
HEAD_NORM_EPS = 1e-6
DEPTH = 1
DEEPNORM_ALPHA = (2.0 * DEPTH) ** 0.25

G_HQ, G_HF, G_HI, G_HG, G_RQ, G_RK, G_RV, G_RG = range(8)
GROUPS_PER_STEP = 2
PROJ_ORDER = tuple(range(GROUPS_PER_STEP, 8)) + tuple(range(GROUPS_PER_STEP))
assert PROJ_ORDER[-1] == G_HF

F32 = jnp.float32
BF16 = jnp.bfloat16

VMEM_LIMIT_BYTES = 56 * 1024 * 1024
VMEM_LIMIT_FUSED_BYTES = 60 * 1024 * 1024

_NT = (((1,), (1,)), ((), ()))


def _layer_norm(x, g, b):
    mu = jnp.mean(x, axis=-1, keepdims=True)
    xc = x - mu
    var = jnp.mean(xc * xc, axis=-1, keepdims=True)
    return xc * lax.rsqrt(var + LN_EPS) * g + b


def _sigmoid(x):
    return 0.5 * jnp.tanh(0.5 * x) + 0.5


def _silu(x):
    return x * _sigmoid(x)


def _forget_lower_bound(hlb_ref):
    a0 = hlb_ref[0:1, :]
    a1 = hlb_ref[1:2, :]
    amax = jnp.maximum(a0, a1)
    e0 = jnp.exp(a0 - amax)
    e1 = jnp.exp(a1 - amax)
    return e0 / (e0 + e1)


def _chunk_consts():
    row = lax.broadcasted_iota(jnp.int32, (CHUNK, 1), 0)
    t_idx = lax.broadcasted_iota(jnp.int32, (CHUNK, CHUNK), 0)
    s_idx = lax.broadcasted_iota(jnp.int32, (CHUNK, CHUNK), 1)
    causal = t_idx >= s_idx
    j = row.astype(F32)
    rel = jnp.maximum(t_idx - s_idx, 0).astype(F32)
    log_gamma = [float(np.log(np.float32(1.0) - np.float32(2.0) ** np.float32(-5.0 - h)))
                 for h in range(RET_HEADS)]
    return dict(
        valid=row >= PAD_FRONT,
        causal=causal,
        tri=jnp.where(causal, 1.0, 0.0).astype(BF16),
        intra_decay=[jnp.where(causal, jnp.exp(rel * lg), 0.0) for lg in log_gamma],
        k_dec=[jnp.exp((CHUNK - 1.0 - j) * lg) for lg in log_gamma],
        q_dec=[jnp.exp((j + 1.0) * lg) for lg in log_gamma],
        chunk_gamma=[float(np.exp(CHUNK * lg)) for lg in log_gamma],
    )


SCAN_UNITS_PER_CHUNK = 28


def _mix_chunks_stages(chunks, lb, consts, hgn_ref, rng_ref, rnb_ref, s_ref, r_ref, out_ref,
                       mask_pad):
    causal = consts["causal"]
    valid = consts["valid"]
    hg_sl = [slice(h * HG_HEAD_DIM, (h + 1) * HG_HEAD_DIM) for h in range(HG_HEADS)]
    ret_sl = [slice(h * RET_HEAD_DIM, (h + 1) * RET_HEAD_DIM) for h in range(RET_HEADS)]

    for ck in chunks:
        f_pre = ck["cols"][G_HF].astype(F32)
        forget = lb + (1.0 - lb) * _sigmoid(f_pre)
        log_f = jnp.log(forget)
        kk = 1.0 - forget
        if mask_pad:
            log_f = jnp.where(valid, log_f, 0.0)
            kk = jnp.where(valid, kk, 0.0)
        lf_hi = log_f.astype(BF16)
        rem = log_f - lf_hi.astype(F32)
        lf_mid = rem.astype(BF16)
        lf_lo = (rem - lf_mid.astype(F32)).astype(BF16)
        ck["kk"] = kk
        ck["cum_terms"] = [jnp.dot(consts["tri"], t, preferred_element_type=F32)
                           for t in (lf_hi, lf_mid, lf_lo)]
        yield

    def advance_states(ck):
        ck["hg_inter"], ck["ret_inter"] = [], []
        for h, sl in enumerate(hg_sl):
            state_t = s_ref[h]
            ck["hg_inter"].append(lax.dot_general(ck["q_in"][:, sl], state_t.astype(BF16), _NT,
                                                  preferred_element_type=F32))
            s_ref[h] = state_t * ck["chunk_decay"][:, sl] + ck["hg_kv"][h]
            if h % 2 == 1:
                yield
        for h in range(RET_HEADS):
            state_t = r_ref[h]
            ck["ret_inter"].append(lax.dot_general(ck["q_dec"][h], state_t.astype(BF16), _NT,
                                                   preferred_element_type=F32))
            r_ref[h] = state_t * consts["chunk_gamma"][h] + ck["ret_kv"][h]
            yield

    for n, ck in enumerate(chunks):
        cols = ck["cols"]
        cum = ck["cum_terms"][0] + ck["cum_terms"][1] + ck["cum_terms"][2]
        ck["chunk_decay"] = jnp.exp(cum[CHUNK - 1:CHUNK, :])
        qs = _silu(cols[G_HQ].astype(F32)) * (HG_HEAD_DIM ** -0.5)
        ck["q_in"] = q_in = (qs * jnp.exp(cum)).astype(BF16)
        k_in = ck["kk"] * jnp.exp(-cum)
        k_out = (k_in * ck["chunk_decay"]).astype(BF16)
        k_in = k_in.astype(BF16)
        ck["v_bf"] = v_bf = cols[G_HI].astype(BF16)
        ck["hg_scores"], ck["hg_kv"] = [], []
        yield
        for h, sl in enumerate(hg_sl):
            ck["hg_scores"].append(lax.dot_general(q_in[:, sl], k_in[:, sl], _NT,
                                                   preferred_element_type=F32))
            v_t = v_bf[:, sl].astype(F32).T.astype(BF16)
            ck["hg_kv"].append(jnp.dot(v_t, k_out[:, sl], preferred_element_type=F32))
            if h % 2 == 1:
                yield

        rq = cols[G_RQ].astype(F32)
        rk = cols[G_RK].astype(F32)
        ck["rv_bf"] = rv_bf = cols[G_RV].astype(BF16)
        cos, sin = ck["cos"], ck["sin"]
        ck["ret_scores"], ck["ret_kv"], ck["q_dec"] = [], [], []
        for h, sl in enumerate(ret_sl):
            lo, mid, hi = sl.start, sl.start + ROT_HALF, sl.stop
            q1, q2 = rq[:, lo:mid], rq[:, mid:hi]
            k1, k2 = rk[:, lo:mid], rk[:, mid:hi]
            qh = jnp.concatenate([q1 * cos - q2 * sin, q1 * sin + q2 * cos], axis=-1)
            kh = (jnp.concatenate([k1 * cos - k2 * sin, k1 * sin + k2 * cos], axis=-1)
                  * (RET_HEAD_DIM ** -0.5))
            if mask_pad:
                kh = jnp.where(valid, kh, 0.0)
            ck["ret_scores"].append(lax.dot_general(qh.astype(BF16), kh.astype(BF16), _NT,
                                                    preferred_element_type=F32))
            k_dec = (kh * consts["k_dec"][h]).astype(BF16)
            ck["q_dec"].append((qh * consts["q_dec"][h]).astype(BF16))
            ck["ret_kv"].append(jnp.dot(rv_bf[:, sl].astype(F32).T.astype(BF16), k_dec,
                                        preferred_element_type=F32))
            yield
        if n == 0:
            yield from advance_states(ck)

    for ck in chunks:
        ck["hg_o"] = [jnp.dot(jnp.where(causal, ck["hg_scores"][h], 0.0).astype(BF16),
                              ck["v_bf"][:, sl], preferred_element_type=F32)
                      for h, sl in enumerate(hg_sl)]
        yield
        ck["ret_o"] = [jnp.dot((ck["ret_scores"][h] * consts["intra_decay"][h]).astype(BF16),
                               ck["rv_bf"][:, sl], preferred_element_type=F32)
                       for h, sl in enumerate(ret_sl)]
        yield
    for ck in chunks[1:]:
        yield from advance_states(ck)

    for ck in chunks:
        cols, rows = ck["cols"], ck["rows"]
        gate = _silu(cols[G_HG].astype(F32))
        for h, sl in enumerate(hg_sl):
            o = ck["hg_o"][h] + ck["hg_inter"][h]
            ms = jnp.mean(o * o, axis=-1, keepdims=True)
            o = o * lax.rsqrt(ms + HEAD_NORM_EPS) * hgn_ref[:, sl] * gate[:, sl]
            out_ref[rows, sl] = o.astype(out_ref.dtype)
            if h % 2 == 1:
                yield
        r_gate = _silu(cols[G_RG].astype(F32))
        for h, sl in enumerate(ret_sl):
            o = ck["ret_o"][h] + ck["ret_inter"][h]
            mu = jnp.mean(o, axis=-1, keepdims=True)
            oc = o - mu
            var = jnp.mean(oc * oc, axis=-1, keepdims=True)
            o = oc * lax.rsqrt(var + HEAD_NORM_EPS) * rng_ref[:, sl] + rnb_ref[:, sl]
            out_ref[rows, HG_WIDTH + sl.start:HG_WIDTH + sl.stop] = (
                o * r_gate[:, sl]).astype(out_ref.dtype)
            yield


def _ln_proj_kernel(x_ref, g_ref, b_ref, w_ref, o_ref, hb_ref):
    @pl.when(pl.program_id(1) == 0)
    def _():
        hb_ref[...] = _layer_norm(x_ref[...], g_ref[...], b_ref[...]).astype(BF16)

    o_ref[...] = jnp.dot(hb_ref[...], w_ref[...], preferred_element_type=F32)


def _ln_proj(x, g, b, w_bf16, *, tm, tn):
    m = x.shape[0]
    return pl.pallas_call(
        _ln_proj_kernel,
        grid=(m // tm, N_PROJ // tn),
        in_specs=[
            pl.BlockSpec((tm, D_MODEL), lambda i, j: (i, 0)),
            pl.BlockSpec((1, D_MODEL), lambda i, j: (0, 0)),
            pl.BlockSpec((1, D_MODEL), lambda i, j: (0, 0)),
            pl.BlockSpec((D_MODEL, tn), lambda i, j: (0, j)),
        ],
        out_specs=pl.BlockSpec((tm, tn), lambda i, j: (i, j)),
        out_shape=jax.ShapeDtypeStruct((m, N_PROJ), F32),
        scratch_shapes=[pltpu.VMEM((tm, D_MODEL), BF16)],
        compiler_params=pltpu.CompilerParams(
            dimension_semantics=("parallel", "arbitrary"),
            vmem_limit_bytes=VMEM_LIMIT_BYTES),
        name="ln_proj_lead",
    )(x, g, b, w_bf16)


def _lead_mixer_kernel(proj_ref, cos_ref, sin_ref, hlb_ref, hgn_ref, rng_ref, rnb_ref,
                       out_ref, s_out_ref, r_out_ref):
    s_out_ref[...] = jnp.zeros_like(s_out_ref)
    r_out_ref[...] = jnp.zeros_like(r_out_ref)
    chunk = dict(cols=[proj_ref[:, gi * GROUP:(gi + 1) * GROUP] for gi in range(8)],
                 cos=cos_ref[...], sin=sin_ref[...], rows=pl.ds(0, CHUNK))
    for _ in _mix_chunks_stages([chunk], _forget_lower_bound(hlb_ref), _chunk_consts(),
                                hgn_ref, rng_ref, rnb_ref, s_out_ref, r_out_ref, out_ref,
                                mask_pad=True):
        pass


def _lead_mixer(proj, cos, sin, hlb, hgn, rng, rnb):
    full = lambda shape: pl.BlockSpec(shape, lambda c: (0,) * len(shape))
    s_shape = (HG_HEADS, HG_HEAD_DIM, HG_HEAD_DIM)
    r_shape = (RET_HEADS, RET_HEAD_DIM, RET_HEAD_DIM)
    return pl.pallas_call(
        _lead_mixer_kernel,
        grid=(1,),
        in_specs=[full((CHUNK, N_PROJ)), full((CHUNK, ROT_HALF)), full((CHUNK, ROT_HALF)),
                  full((2, HG_WIDTH)), full((1, HG_WIDTH)), full((1, RET_WIDTH)),
                  full((1, RET_WIDTH))],
        out_specs=[full((CHUNK, HG_WIDTH + RET_WIDTH)), full(s_shape), full(r_shape)],
        out_shape=[jax.ShapeDtypeStruct((CHUNK, HG_WIDTH + RET_WIDTH), BF16),
                   jax.ShapeDtypeStruct(s_shape, F32),
                   jax.ShapeDtypeStruct(r_shape, F32)],
        compiler_params=pltpu.CompilerParams(
            dimension_semantics=("arbitrary",),
            vmem_limit_bytes=VMEM_LIMIT_BYTES),
        name="mixer_lead",
    )(proj, cos, sin, hlb, hgn, rng, rnb)


def _proj_mixer_kernel(x_ref, gin_ref, bin_ref, w_ref, cos_ref, sin_ref, hlb_ref, hgn_ref,
                       rng_ref, rnb_ref, s_in_ref, r_in_ref, out_ref,
                       hb_ref, acc_ref, pf0_ref, pf1_ref, pb0_ref, pb1_ref, s_ref, r_ref,
                       *, tm):
    i = pl.program_id(0)
    j = pl.program_id(1)
    n_blocks = pl.num_programs(0) - 1
    groups_per_step = GROUPS_PER_STEP
    n_j = N_PROJ // (groups_per_step * GROUP)
    chunks_per_step = tm // CHUNK // n_j

    @pl.when((i == 0) & (j == 0))
    def _():
        pf1_ref[...] = jnp.zeros_like(pf1_ref)
        pb1_ref[...] = jnp.zeros_like(pb1_ref)
        s_ref[...] = jnp.zeros_like(s_ref)
        r_ref[...] = jnp.zeros_like(r_ref)

    @pl.when((i == 1) & (j == 0))
    def _():
        s_ref[...] = s_in_ref[...]
        r_ref[...] = r_in_ref[...]

    @pl.when((j == 0) & (i < n_blocks))
    def _():
        for r0 in range(0, tm, CHUNK):
            rows = pl.ds(r0, CHUNK)
            hb_ref[rows, :] = _layer_norm(x_ref[rows, :], gin_ref[...], bin_ref[...]).astype(BF16)

    n_cols = k_cols = acc_ref.shape[1]

    def project_pieces(pf_ref, pb_ref):
        def make(g, p, k):
            def run():
                c0 = g * GROUP + p * n_cols
                part = jnp.dot(hb_ref[:, k * k_cols:(k + 1) * k_cols],
                               w_ref[k * k_cols:(k + 1) * k_cols, c0:c0 + n_cols],
                               preferred_element_type=F32)
                if k == 0:
                    acc_ref[...] = part
                elif k < D_MODEL // k_cols - 1:
                    acc_ref[...] += part
                else:
                    res = acc_ref[...] + part
                    pb_ref[j * groups_per_step + g, :, p * n_cols:(p + 1) * n_cols] = (
                        res.astype(BF16))
                    if g == groups_per_step - 1:
                        pf_ref[:, p * n_cols:(p + 1) * n_cols] = res
            return run
        return [make(g, p, k) for g in range(groups_per_step) for p in range(GROUP // n_cols)
                for k in range(D_MODEL // k_cols)]

    def scan_chunks(pf_ref, pb_ref):
        chunks = []
        for t in range(chunks_per_step):
            rows = pl.ds(pl.multiple_of((j * chunks_per_step + t) * CHUNK, CHUNK), CHUNK)
            cols = [None] * 8
            for pos, gi in enumerate(PROJ_ORDER):
                cols[gi] = pf_ref[rows, :] if gi == G_HF else pb_ref[pos, rows, :]
            chunks.append(dict(cols=cols, cos=cos_ref[rows, :], sin=sin_ref[rows, :], rows=rows))
        return _mix_chunks_stages(chunks, _forget_lower_bound(hlb_ref), _chunk_consts(),
                                  hgn_ref, rng_ref, rnb_ref, s_ref, r_ref, out_ref,
                                  mask_pad=False)

    def fused_step(wpf, wpb, rpf, rpb):
        units = SCAN_UNITS_PER_CHUNK * chunks_per_step
        pieces = project_pieces(wpf, wpb)
        scan = scan_chunks(rpf, rpb)
        done = 0
        for n, run_piece in enumerate(pieces):
            run_piece()
            target = (n + 1) * units // len(pieces)
            while done < target:
                next(scan, None)
                done += 1
        for _ in scan:
            pass

    even = i % 2 == 0

    @pl.when(even & (i < n_blocks))
    def _():
        fused_step(pf0_ref, pb0_ref, pf1_ref, pb1_ref)

    @pl.when(jnp.logical_not(even) & (i < n_blocks))
    def _():
        fused_step(pf1_ref, pb1_ref, pf0_ref, pb0_ref)

    @pl.when(i == n_blocks)
    def _():
        for _ in scan_chunks(pf1_ref, pb1_ref):
            pass


def _proj_mixer(x, gin, bin_, w_bf16, cos, sin, hlb, hgn, rng, rnb, s_in, r_in, *, tm, piece):
    m = x.shape[0]
    n_blocks = m // tm
    assert n_blocks % 2 == 0
    tn = GROUPS_PER_STEP * GROUP
    n_j = N_PROJ // tn
    const = lambda shape: pl.BlockSpec(shape, lambda i, j: (0,) * len(shape))
    single = lambda shape: pl.BlockSpec(shape, lambda i, j: (0,) * len(shape),
                                        pipeline_mode=pl.Buffered(1))
    prev = lambda i, j: (jnp.maximum(i - 1, 0), 0)
    return pl.pallas_call(
        functools.partial(_proj_mixer_kernel, tm=tm),
        grid=(n_blocks + 1, n_j),
        in_specs=[
            pl.BlockSpec((tm, D_MODEL), lambda i, j: (jnp.minimum(i, n_blocks - 1), 0)),
            const((1, D_MODEL)), const((1, D_MODEL)),
            pl.BlockSpec((D_MODEL, tn), lambda i, j: (0, (j + 1) % n_j)),
            pl.BlockSpec((tm, ROT_HALF), prev),
            pl.BlockSpec((tm, ROT_HALF), prev),
            const((2, HG_WIDTH)), const((1, HG_WIDTH)), const((1, RET_WIDTH)), const((1, RET_WIDTH)),
            single(s_in.shape), single(r_in.shape),
        ],
        out_specs=pl.BlockSpec((tm, HG_WIDTH + RET_WIDTH), prev),
        out_shape=jax.ShapeDtypeStruct((m, HG_WIDTH + RET_WIDTH), BF16),
        scratch_shapes=[
            pltpu.VMEM((tm, D_MODEL), BF16),
            pltpu.VMEM((tm, piece), F32),
            pltpu.VMEM((tm, GROUP), F32), pltpu.VMEM((tm, GROUP), F32),
            pltpu.VMEM((8, tm, GROUP), BF16), pltpu.VMEM((8, tm, GROUP), BF16),
            pltpu.VMEM(s_in.shape, F32),
            pltpu.VMEM(r_in.shape, F32),
        ],
        compiler_params=pltpu.CompilerParams(
            dimension_semantics=("arbitrary", "arbitrary"),
            vmem_limit_bytes=VMEM_LIMIT_FUSED_BYTES),
        name="proj_mixer",
    )(x, gin, bin_, w_bf16, cos, sin, hlb, hgn, rng, rnb, s_in, r_in)


def _out_proj_kernel(mix_ref, x_ref, gin_ref, bin_ref, w_ref, g1_ref, b1_ref, hb_ref):
    h = _layer_norm(x_ref[...], gin_ref[...], bin_ref[...])
    mixed = jnp.dot(mix_ref[...], w_ref[...], preferred_element_type=F32)
    h1 = _layer_norm(DEEPNORM_ALPHA * h + mixed, g1_ref[...], b1_ref[...])
    hb_ref[...] = h1.astype(BF16)


def _out_proj(mix, x, gin, bin_, w_bf16, g1, b1, *, tm):
    m = x.shape[0]
    vec = pl.BlockSpec((1, D_MODEL), lambda i: (0, 0))
    row = pl.BlockSpec((tm, D_MODEL), lambda i: (i, 0))
    return pl.pallas_call(
        _out_proj_kernel,
        grid=(m // tm,),
        in_specs=[row, row, vec, vec,
                  pl.BlockSpec((D_MODEL, D_MODEL), lambda i: (0, 0)), vec, vec],
        out_specs=row,
        out_shape=jax.ShapeDtypeStruct((m, D_MODEL), BF16),
        compiler_params=pltpu.CompilerParams(
            dimension_semantics=("parallel",),
            vmem_limit_bytes=VMEM_LIMIT_BYTES),
        name="out_proj_ln1",
    )(mix, x, gin, bin_, w_bf16, g1, b1)


def _ffn_kernel(hb_ref, wg_ref, wu_ref, wd_ref, g2_ref, b2_ref, o_ref):
    k = pl.program_id(1)

    @pl.when(k == 0)
    def _():
        o_ref[...] = jnp.zeros_like(o_ref)

    hb = hb_ref[...]
    gate = jnp.dot(hb, wg_ref[...].astype(BF16), preferred_element_type=F32)
    up = jnp.dot(hb, wu_ref[...].astype(BF16), preferred_element_type=F32)
    act = (_silu(gate) * up).astype(BF16)
    o_ref[...] += jnp.dot(act, wd_ref[...].astype(BF16), preferred_element_type=F32)

    @pl.when(k == pl.num_programs(1) - 1)
    def _():
        for r0 in range(0, o_ref.shape[0], 128):
            rows = pl.ds(r0, 128)
            o_ref[rows, :] = _layer_norm(
                DEEPNORM_ALPHA * hb_ref[rows, :].astype(F32) + o_ref[rows, :],
                g2_ref[...], b2_ref[...])


def _ffn(hb, wg, wu, wd, g2, b2, *, tm, tf):
    m = hb.shape[0]
    vec = pl.BlockSpec((1, D_MODEL), lambda i, k: (0, 0))
    row = pl.BlockSpec((tm, D_MODEL), lambda i, k: (i, 0))
    return pl.pallas_call(
        _ffn_kernel,
        grid=(m // tm, D_FF // tf),
        in_specs=[row,
                  pl.BlockSpec((D_MODEL, tf), lambda i, k: (0, k)),
                  pl.BlockSpec((D_MODEL, tf), lambda i, k: (0, k)),
                  pl.BlockSpec((tf, D_MODEL), lambda i, k: (k, 0)),
                  vec, vec],
        out_specs=row,
        out_shape=jax.ShapeDtypeStruct((m, D_MODEL), F32),
        compiler_params=pltpu.CompilerParams(
            dimension_semantics=("parallel", "arbitrary"),
            vmem_limit_bytes=VMEM_LIMIT_FUSED_BYTES),
        name="ffn_ln2",
    )(hb, wg, wu, wd, g2, b2)


def kernel(x, meta_tokens, ln_in_g, ln_in_b, hg_lower_bounds, w_in, hg_norm_g, ret_norm_g,
           ret_norm_b, w_out, ln1_g, ln1_b, w_gate, w_up, w_down, ln2_g, ln2_b):
    b_sz, seq, _ = x.shape
    assert b_sz == 1 and seq % CHUNK == 0 and DEPTH == 1
    x2 = x[0]
    row = lambda p: p.reshape(1, -1).astype(F32)
    gin, bin_ = row(ln_in_g), row(ln_in_b)

    length = PAD_FRONT + N_META + seq
    pos = (jnp.arange(length, dtype=jnp.int32) - PAD_FRONT).astype(F32)
    inv_freq = ROPE_BASE ** (-jnp.arange(ROT_HALF, dtype=F32) / ROT_HALF)
    ang = pos[:, None] * inv_freq[None, :]
    cos, sin = jnp.cos(ang), jnp.sin(ang)

    w_in_b = w_in[0].astype(BF16)
    w_out_b = w_out[0].astype(BF16)
    hlb = hg_lower_bounds.astype(F32)
    hgn, rng, rnb = row(hg_norm_g[0]), row(ret_norm_g[0]), row(ret_norm_b[0])

    x_lead = jnp.concatenate([jnp.zeros((PAD_FRONT, D_MODEL), x.dtype),
                              meta_tokens.astype(x.dtype)], axis=0)
    proj_lead = _ln_proj(x_lead, gin, bin_, w_in_b, tm=CHUNK, tn=2048)
    _, s1, r1 = _lead_mixer(proj_lead, cos[:CHUNK], sin[:CHUNK], hlb, hgn, rng, rnb)

    mix = _proj_mixer(x2, gin, bin_, w_in_b, cos[CHUNK:], sin[CHUNK:], hlb, hgn, rng, rnb,
                      s1, r1, tm=512, piece=512)
    h1b = _out_proj(mix, x2, gin, bin_, w_out_b, row(ln1_g[0]), row(ln1_b[0]), tm=256)
    out = _ffn(h1b, w_gate[0], w_up[0], w_down[0], row(ln2_g[0]), row(ln2_b[0]), tm=1024, tf=512)
    return out[None]
```
